```python
import jax, jax.numpy as jnp
from jax import lax
import numpy as np

D_MODEL = 2048
BATCH = 2
SEQ = 4096
DEPTH = 2

N_MIXERS = 2
HEAD_DIM = 128
ATTN_HEADS = D_MODEL // HEAD_DIM
DILATED_GROUPS = ((128, 1), (512, 4), (2048, 16))
N_DIL = len(DILATED_GROUPS)
ROT_DIM = HEAD_DIM // 4
ROPE_THETA = 500000.0
FNET_WIDTH = D_MODEL
FNET_GROUPS = 4
FNET_GROUP_CH = FNET_WIDTH // FNET_GROUPS
N_EXPERT_GROUPS = 4
EXPERTS_PER_GROUP = 8
N_EXPERTS = N_EXPERT_GROUPS * EXPERTS_PER_GROUP
EXPERT_DIM = D_MODEL // 4
TOP_K_FINE = 2
MOE_BLOCK = 128
LN_EPS = 1e-5
NEG_INF = -1e30
DEEPNORM_ALPHA = (2 * DEPTH) ** 0.25
DEEPNORM_BETA = (8 * DEPTH) ** -0.25
N_ATTN_LAYERS = (DEPTH + 1) // 2
N_FNET_LAYERS = DEPTH // 2

kernel_name = "hybrid_dilated_attn_fnet_hier_moe_deepnorm_adaln"


def layer_norm(x, g, b):
    xf = x.astype(jnp.float32)
    mu = jnp.mean(xf, axis=-1, keepdims=True)
    var = jnp.mean(jnp.square(xf - mu), axis=-1, keepdims=True)
    y = (xf - mu) * lax.rsqrt(var + LN_EPS)
    return (y * g.astype(jnp.float32) + b.astype(jnp.float32)).astype(x.dtype)


def adaln(c, w, b):
    m = jax.nn.silu(c) @ w + b
    shift, scale, gate = jnp.split(m, 3, axis=-1)
    return shift[:, None, :], scale[:, None, :], gate[:, None, :]


def rotary_tables(positions, dtype):
    inv_freq = ROPE_THETA ** (-jnp.arange(0, ROT_DIM, 2, dtype=jnp.float32) / ROT_DIM)
    ang = positions.astype(jnp.float32)[..., None] * inv_freq
    return (jnp.cos(ang)[:, :, None, :].astype(dtype),
            jnp.sin(ang)[:, :, None, :].astype(dtype))


def partial_rotary(t, cos, sin):
    half = ROT_DIM // 2
    t1, t2, rest = t[..., :half], t[..., half:ROT_DIM], t[..., ROT_DIM:]
    return jnp.concatenate([t1 * cos - t2 * sin, t2 * cos + t1 * sin, rest], axis=-1)


def stride_split(t, d):
    bsz, s = t.shape[:2]
    rest = t.shape[2:]
    t = jnp.moveaxis(t.reshape(bsz, s // d, d, *rest), 2, 1)
    return t.reshape(bsz * d, s // d, *rest)


def stride_merge(t, bsz, d):
    n, l = t.shape[:2]
    rest = t.shape[2:]
    t = jnp.moveaxis(t.reshape(bsz, d, l, *rest), 1, 2)
    return t.reshape(bsz, l * d, *rest)


def banded_attention(q, k, v, radius):
    n, l, h, dh = q.shape
    blk = radius
    nb = -(-l // blk)
    lp = nb * blk
    qb = jnp.pad(q, ((0, 0), (0, lp - l), (0, 0), (0, 0))).reshape(n, nb, blk, h, dh)

    def windows(t):
        tp = jnp.pad(t, ((0, 0), (blk, lp - l + blk), (0, 0), (0, 0))).reshape(n, nb + 2, blk, h, dh)
        return jnp.concatenate([tp[:, :-2], tp[:, 1:-1], tp[:, 2:]], axis=2)

    kw, vw = windows(k), windows(v)
    s = jnp.einsum('nbqhd,nbkhd->nbhqk', qb, kw, preferred_element_type=jnp.float32) * (dh ** -0.5)
    qpos = jnp.arange(nb)[:, None] * blk + jnp.arange(blk)[None, :]
    kpos = jnp.arange(nb)[:, None] * blk - blk + jnp.arange(3 * blk)[None, :]
    valid = ((jnp.abs(qpos[:, :, None] - kpos[:, None, :]) <= radius)
             & (kpos >= 0)[:, None, :] & (kpos < l)[:, None, :])
    s = jnp.where(valid[None, :, None], s, NEG_INF)
    m = jnp.max(s, axis=-1, keepdims=True)
    p = jnp.exp(s - m)
    den = jnp.sum(p, axis=-1, keepdims=True)
    o = jnp.einsum('nbhqk,nbkhd->nbqhd', p, vw.astype(jnp.float32)) / jnp.swapaxes(den, 2, 3)
    lse = jnp.swapaxes((m + jnp.log(den))[..., 0], 2, 3)
    return o.reshape(n, lp, h, dh)[:, :l], lse.reshape(n, lp, h)[:, :l]


def dilated_attention(h, w_qkv, w_o, cos, sin):
    bsz, s, _ = h.shape
    qkv = (h @ w_qkv).reshape(bsz, s, N_DIL, 3, ATTN_HEADS, HEAD_DIM)
    outs, lses = [], []
    for g, (window, dil) in enumerate(DILATED_GROUPS):
        radius = window // (2 * dil)
        q = partial_rotary(qkv[:, :, g, 0], cos, sin)
        k = partial_rotary(qkv[:, :, g, 1], cos, sin)
        v = qkv[:, :, g, 2]
        o_g, lse_g = banded_attention(stride_split(q, dil), stride_split(k, dil),
                                      stride_split(v, dil), radius)
        outs.append(stride_merge(o_g, bsz, dil))
        lses.append(stride_merge(lse_g, bsz, dil))
    wts = jax.nn.softmax(jnp.stack(lses), axis=0)
    o = jnp.einsum('gbsh,gbshd->bshd', wts, jnp.stack(outs)).astype(h.dtype)
    return o.reshape(bsz, s, ATTN_HEADS * HEAD_DIM) @ w_o


def fourier_mixer(h, w_in, w_out):
    bsz, s, _ = h.shape
    u = (h @ w_in).reshape(bsz, s, FNET_GROUPS, FNET_GROUP_CH).astype(jnp.float32)
    y = jnp.fft.fft2(u, axes=(1, 3), norm='ortho').real
    return y.astype(h.dtype).reshape(bsz, s, FNET_WIDTH) @ w_out


def hier_moe(h, wr1, br1, wr2, br2, w_gate, w_up, w_down):
    bsz, s, d = h.shape
    t = h.reshape(-1, d)
    n_tok = t.shape[0]
    pc = jax.nn.softmax((t @ wr1).astype(jnp.float32) + br1.astype(jnp.float32), axis=-1)
    pg, gi = lax.top_k(pc, 1)
    lf = ((t @ wr2).astype(jnp.float32) + br2.astype(jnp.float32)).reshape(n_tok, N_EXPERT_GROUPS, EXPERTS_PER_GROUP)
    lf_sel = jnp.take_along_axis(lf, gi[:, :, None], axis=1)[:, 0]
    pk, ek = lax.top_k(jax.nn.softmax(lf_sel, axis=-1), TOP_K_FINE)
    gates = pg * (pk / jnp.sum(pk, axis=-1, keepdims=True))
    eid = gi * EXPERTS_PER_GROUP + ek

    tk = n_tok * TOP_K_FINE
    e_flat = eid.reshape(-1)
    g_flat = gates.reshape(-1)
    tok = jnp.arange(tk, dtype=jnp.int32) // TOP_K_FINE
    order = jnp.argsort(e_flat)
    e_s, tok_s, g_s = e_flat[order], tok[order], g_flat[order]
    counts = jnp.bincount(e_flat, length=N_EXPERTS)
    starts = jnp.cumsum(counts) - counts
    padded = (counts + MOE_BLOCK - 1) // MOE_BLOCK * MOE_BLOCK
    pends = jnp.cumsum(padded)
    pstarts = pends - padded
    dest = pstarts[e_s] + (jnp.arange(tk) - starts[e_s])
    n_blocks = -(-tk // MOE_BLOCK) + N_EXPERTS
    n_rows = n_blocks * MOE_BLOCK
    row_tok = jnp.zeros((n_rows,), jnp.int32).at[dest].set(tok_s)
    row_gate = jnp.zeros((n_rows,), t.dtype).at[dest].set(g_s.astype(t.dtype))
    block_e = jnp.clip(jnp.searchsorted(pends, jnp.arange(n_blocks) * MOE_BLOCK, side='right'), 0, N_EXPERTS - 1)

    def run_block(args):
        tok_b, gate_b, e = args
        xb = t[tok_b]
        a = jax.nn.silu(xb @ w_gate[e]) * (xb @ w_up[e])
        return (a @ w_down[e]) * gate_b[:, None]

    ys = lax.map(run_block, (row_tok.reshape(n_blocks, MOE_BLOCK),
                             row_gate.reshape(n_blocks, MOE_BLOCK), block_e))
    out = jnp.zeros_like(t).at[row_tok].add(ys.reshape(n_rows, d))
    return out.reshape(bsz, s, d)


def setup_inputs(seed: int = 0) -> dict:
    key = jax.random.key(seed)
    ks = jax.random.split(key, 20)

    def nrm(k, shape, scale):
        return jax.random.normal(k, shape, jnp.float32) * scale

    hd_total = ATTN_HEADS * HEAD_DIM
    qkv_scale = jnp.array([1.0, 1.0, DEEPNORM_BETA], jnp.float32).reshape(1, 1, 1, 3, 1)
    return {
        "x": nrm(ks[0], (BATCH, SEQ, D_MODEL), 1.0),
        "c": nrm(ks[1], (BATCH, D_MODEL), 1.0),
        "positions": jnp.broadcast_to(jnp.arange(SEQ, dtype=jnp.int32), (BATCH, SEQ)),
        "ada_w": nrm(ks[2], (DEPTH, 2, D_MODEL, 3 * D_MODEL), D_MODEL ** -0.5),
        "ada_b": nrm(ks[3], (DEPTH, 2, 3 * D_MODEL), 0.02),
        "attn_w_qkv": (nrm(ks[4], (N_ATTN_LAYERS, D_MODEL, N_DIL, 3, hd_total), D_MODEL ** -0.5)
                       * qkv_scale).reshape(N_ATTN_LAYERS, D_MODEL, N_DIL * 3 * hd_total),
        "attn_w_o": nrm(ks[5], (N_ATTN_LAYERS, hd_total, D_MODEL), hd_total ** -0.5 * DEEPNORM_BETA),
        "fnet_w_in": nrm(ks[6], (N_FNET_LAYERS, D_MODEL, FNET_WIDTH), D_MODEL ** -0.5 * DEEPNORM_BETA),
        "fnet_w_out": nrm(ks[7], (N_FNET_LAYERS, FNET_WIDTH, D_MODEL), FNET_WIDTH ** -0.5 * DEEPNORM_BETA),
        "ln_g": 1.0 + nrm(ks[8], (DEPTH, 2, D_MODEL), 0.02),
        "ln_b": nrm(ks[9], (DEPTH, 2, D_MODEL), 0.02),
        "router_coarse_w": nrm(ks[10], (DEPTH, D_MODEL, N_EXPERT_GROUPS), D_MODEL ** -0.5),
        "router_coarse_b": nrm(ks[11], (DEPTH, N_EXPERT_GROUPS), 0.01),
        "router_fine_w": nrm(ks[12], (DEPTH, D_MODEL, N_EXPERTS), D_MODEL ** -0.5),
        "router_fine_b": nrm(ks[13], (DEPTH, N_EXPERTS), 0.01),
        "expert_w_gate": nrm(ks[14], (DEPTH, N_EXPERTS, D_MODEL, EXPERT_DIM), D_MODEL ** -0.5 * DEEPNORM_BETA),
        "expert_w_up": nrm(ks[15], (DEPTH, N_EXPERTS, D_MODEL, EXPERT_DIM), D_MODEL ** -0.5 * DEEPNORM_BETA),
        "expert_w_down": nrm(ks[16], (DEPTH, N_EXPERTS, EXPERT_DIM, D_MODEL), EXPERT_DIM ** -0.5 * DEEPNORM_BETA),
    }


def reference(x, c, positions, ada_w, ada_b, attn_w_qkv, attn_w_o, fnet_w_in, fnet_w_out,
              ln_g, ln_b, router_coarse_w, router_coarse_b, router_fine_w, router_fine_b,
              expert_w_gate, expert_w_up, expert_w_down):
    cos, sin = rotary_tables(positions, x.dtype)
    for i in range(DEPTH):
        j = i // N_MIXERS
        shift, scale, gate = adaln(c, ada_w[i, 0], ada_b[i, 0])
        h = x * (1.0 + scale) + shift
        if i % N_MIXERS == 0:
            y = dilated_attention(h, attn_w_qkv[j], attn_w_o[j], cos, sin)
        else:
            y = fourier_mixer(h, fnet_w_in[j], fnet_w_out[j])
        x = layer_norm(DEEPNORM_ALPHA * x + gate * y, ln_g[i, 0], ln_b[i, 0])
        shift, scale, gate = adaln(c, ada_w[i, 1], ada_b[i, 1])
        h = x * (1.0 + scale) + shift
        y = hier_moe(h, router_coarse_w[i], router_coarse_b[i], router_fine_w[i], router_fine_b[i],
                     expert_w_gate[i], expert_w_up[i], expert_w_down[i])
        x = layer_norm(DEEPNORM_ALPHA * x + gate * y, ln_g[i, 1], ln_b[i, 1])
    return x
```

```python
import functools
import math

import numpy as np
import jax
import jax.numpy as jnp
from jax import lax
from jax.experimental import pallas as pl
from jax.experimental.pallas import tpu as pltpu

F32 = jnp.float32
BF16 = jnp.bfloat16

D_MODEL = 2048
BATCH = 2
SEQ = 4096
TOKENS = BATCH * SEQ
DEPTH = 2
HEAD_DIM = 128
HEADS = D_MODEL // HEAD_DIM
DILATIONS = (1, 4, 16)
RADIUS = 64
N_DIL = 3
QKV_COLS = N_DIL * 3 * D_MODEL
ROT_DIM = HEAD_DIM // 4
ROT_HALF = ROT_DIM // 2
ROPE_THETA = 500000.0
FNET_GROUPS = 4
FNET_CH = D_MODEL // FNET_GROUPS
N_GROUPS = 4
EPG = 8
N_EXPERTS = N_GROUPS * EPG
EXPERT_DIM = D_MODEL // 4
LN_EPS = 1e-5
NEG_INF = -1e30
ALPHA = (2 * DEPTH) ** 0.25

LANES = 128
SLABS = D_MODEL // LANES
VMEM_LIMIT = 56 * 1024 * 1024

MOE_BLK = 256
MOE_NB = (2 * TOKENS) // MOE_BLK + N_EXPERTS
MOE_ROWS = MOE_NB * MOE_BLK


def _cparams(sem):
    return pltpu.CompilerParams(dimension_semantics=sem, vmem_limit_bytes=VMEM_LIMIT)


def _adaln_kernel(c_ref, w_ref, b_ref, o_ref):
    c = c_ref[...]
    sc = (c * (1.0 / (1.0 + jnp.exp(-c)))).astype(BF16)
    m = jnp.dot(sc, w_ref[0].astype(BF16), preferred_element_type=F32)
    o_ref[0] = m + b_ref[0]


def _adaln(c8, w4, b4):
    tn = 768
    n = 3 * D_MODEL
    return pl.pallas_call(
        _adaln_kernel,
        grid=(4, n // tn),
        in_specs=[pl.BlockSpec((8, D_MODEL), lambda s, j: (0, 0)),
                  pl.BlockSpec((1, D_MODEL, tn), lambda s, j: (s, 0, j)),
                  pl.BlockSpec((1, 1, tn), lambda s, j: (s, 0, j))],
        out_specs=pl.BlockSpec((1, 8, tn), lambda s, j: (s, 0, j)),
        out_shape=jax.ShapeDtypeStruct((4, 8, n), F32),
        compiler_params=_cparams(("arbitrary", "arbitrary")),
        name="adaln",
    )(c8, w4, b4)


def _rot_tab_kernel(pos_ref, invf_ref, o_ref):
    ang = pos_ref[...].astype(F32) * invf_ref[...]
    lane = lax.broadcasted_iota(jnp.int32, ang.shape, 1)
    c = jnp.where(lane < ROT_DIM, jnp.cos(ang), 1.0)
    s = jnp.sin(ang)
    s_lo = jnp.where(lane < ROT_HALF, -s, 0.0)
    s_hi = jnp.where((lane >= ROT_HALF) & (lane < ROT_DIM), s, 0.0)
    qs = HEAD_DIM ** -0.5
    o_ref[0] = c * qs
    o_ref[1] = s_lo * qs
    o_ref[2] = s_hi * qs
    o_ref[3] = c
    o_ref[4] = s_lo
    o_ref[5] = s_hi


def _rot_tables(pos_col, invf):
    tm = 1024
    return pl.pallas_call(
        _rot_tab_kernel,
        grid=(TOKENS // tm,),
        in_specs=[pl.BlockSpec((tm, 1), lambda i: (i, 0)),
                  pl.BlockSpec((1, LANES), lambda i: (0, 0))],
        out_specs=pl.BlockSpec((6, tm, LANES), lambda i: (0, i, 0)),
        out_shape=jax.ShapeDtypeStruct((6, TOKENS, LANES), F32),
        compiler_params=_cparams(("arbitrary",)),
        name="rot_tables",
    )(pos_col, invf)


QKV_TM = 1024
QKV_TN = 512


def _rotate(t, c, s_lo, s_hi):
    return (t * c + pltpu.roll(t, LANES - ROT_HALF, axis=1) * s_lo
            + pltpu.roll(t, ROT_HALF, axis=1) * s_hi)


def _qkv_kernel(x_ref, sc_ref, sh_ref, w_ref, tab_ref, o_ref, h_ref):
    j = pl.program_id(1)

    @pl.when(j == 0)
    def _():
        h_ref[...] = (x_ref[...] * (1.0 + sc_ref[0]) + sh_ref[0]).astype(BF16)

    acc = jnp.dot(h_ref[...], w_ref[...].astype(BF16), preferred_element_type=F32)
    which = (j * QKV_TN // D_MODEL) % 3

    def rot(base):
        for hc in range(QKV_TN // LANES):
            sl = slice(hc * LANES, (hc + 1) * LANES)
            o_ref[:, sl] = _rotate(acc[:, sl], tab_ref[base], tab_ref[base + 1],
                                   tab_ref[base + 2]).astype(BF16)

    @pl.when(which == 0)
    def _():
        rot(0)

    @pl.when(which == 1)
    def _():
        rot(3)

    @pl.when(which == 2)
    def _():
        o_ref[...] = acc.astype(BF16)


def _qkv(x2, scale, shift, w_qkv, tabs):
    tm, tn = QKV_TM, QKV_TN
    per_b = SEQ // tm
    return pl.pallas_call(
        _qkv_kernel,
        grid=(TOKENS // tm, QKV_COLS // tn),
        in_specs=[pl.BlockSpec((tm, D_MODEL), lambda i, j: (i, 0)),
                  pl.BlockSpec((1, 1, D_MODEL), lambda i, j: (i // per_b, 0, 0)),
                  pl.BlockSpec((1, 1, D_MODEL), lambda i, j: (i // per_b, 0, 0)),
                  pl.BlockSpec((D_MODEL, tn), lambda i, j: (0, j)),
                  pl.BlockSpec((6, tm, LANES), lambda i, j: (0, i, 0))],
        out_specs=pl.BlockSpec((tm, tn), lambda i, j: (i, j)),
        out_shape=jax.ShapeDtypeStruct((TOKENS, QKV_COLS), BF16),
        scratch_shapes=[pltpu.VMEM((tm, D_MODEL), BF16)],
        compiler_params=_cparams(("arbitrary", "arbitrary")),
        name="qkv_proj",
    )(x2, scale, shift, w_qkv, tabs)


ATT_TQ = 128
ATT_TK = ATT_TQ + 2 * RADIUS


def _attn_kernel(q_ref, kp_ref, kc_ref, kn_ref, vp_ref, vc_ref, vn_ref, o_ref, lse_ref, kbuf, vbuf, *, seq_len):
    i = pl.program_id(2)
    kbuf[0:RADIUS] = kp_ref[0]
    kbuf[RADIUS:RADIUS + ATT_TQ] = kc_ref[0]
    kbuf[RADIUS + ATT_TQ:ATT_TK] = kn_ref[0]
    vbuf[0:RADIUS] = vp_ref[0]
    vbuf[RADIUS:RADIUS + ATT_TQ] = vc_ref[0]
    vbuf[RADIUS + ATT_TQ:ATT_TK] = vn_ref[0]

    qpos = i * ATT_TQ + lax.broadcasted_iota(jnp.int32, (ATT_TQ, ATT_TK), 0)
    kpos = i * ATT_TQ - RADIUS + lax.broadcasted_iota(jnp.int32, (ATT_TQ, ATT_TK), 1)
    valid = (jnp.abs(qpos - kpos) <= RADIUS) & (kpos >= 0) & (kpos < seq_len)
    bias = jnp.where(valid, 0.0, NEG_INF).astype(F32)
    lane = lax.broadcasted_iota(jnp.int32, (ATT_TQ, LANES), 1)
    lse_all = jnp.zeros((ATT_TQ, LANES), F32)
    for h in range(HEADS):
        sl = slice(h * HEAD_DIM, (h + 1) * HEAD_DIM)
        s = lax.dot_general(q_ref[0, :, sl], kbuf[:, sl], (((1,), (1,)), ((), ())),
                            preferred_element_type=F32) + bias
        m = jnp.max(s, axis=-1, keepdims=True)
        p = jnp.exp(s - m)
        den = jnp.sum(p, axis=-1, keepdims=True)
        o = jnp.dot(p.astype(BF16), vbuf[:, sl], preferred_element_type=F32) / den
        o_ref[0, :, sl] = o.astype(BF16)
        lse_all = jnp.where(lane == h, m + jnp.log(den), lse_all)
    lse_ref[0] = lse_all


def _attention_group(qkv, g, dil):
    seq_len = SEQ // dil
    tq = ATT_TQ
    sub = tq // RADIUS
    n_sub = seq_len // RADIUS
    qv = qkv.reshape(BATCH, seq_len, dil * QKV_COLS)
    cpb = QKV_COLS // D_MODEL

    def col(r, which):
        return r * cpb + g * 3 + which

    def cur(which):
        return pl.BlockSpec((1, tq, D_MODEL), lambda b, r, i: (b, i, col(r, which)))

    def prev(which):
        return pl.BlockSpec((1, RADIUS, D_MODEL), lambda b, r, i: (b, jnp.maximum(i * sub - 1, 0), col(r, which)))

    def nxt(which):
        return pl.BlockSpec((1, RADIUS, D_MODEL),
                            lambda b, r, i: (b, jnp.minimum((i + 1) * sub, n_sub - 1), col(r, which)))

    o, lse = pl.pallas_call(
        functools.partial(_attn_kernel, seq_len=seq_len),
        grid=(BATCH, dil, seq_len // tq),
        in_specs=[cur(0), prev(1), cur(1), nxt(1), prev(2), cur(2), nxt(2)],
        out_specs=[pl.BlockSpec((1, tq, D_MODEL), lambda b, r, i: (b, i, r)),
                   pl.BlockSpec((1, tq, LANES), lambda b, r, i: (b, i, r))],
        out_shape=[jax.ShapeDtypeStruct((BATCH, seq_len, dil * D_MODEL), BF16),
                   jax.ShapeDtypeStruct((BATCH, seq_len, dil * LANES), F32)],
        scratch_shapes=[pltpu.VMEM((ATT_TK, D_MODEL), BF16), pltpu.VMEM((ATT_TK, D_MODEL), BF16)],
        compiler_params=_cparams(("arbitrary", "arbitrary", "arbitrary")),
        name=f"attn_dil{dil}",
    )(qv, qv, qv, qv, qv, qv, qv)
    return o.reshape(TOKENS, D_MODEL), lse.reshape(TOKENS, LANES)


def _deepnorm(x, y, gate, g, b):
    z = ALPHA * x + gate * y
    mu = jnp.mean(z, axis=-1, keepdims=True)
    zc = z - mu
    var = jnp.mean(zc * zc, axis=-1, keepdims=True)
    return zc * lax.rsqrt(var + LN_EPS) * g + b


def _split_bf16(a):
    hi = a.astype(BF16)
    lo = (a - hi.astype(F32)).astype(BF16)
    return hi, lo


def _router_logits(h, wr, br):
    h_hi, h_lo = _split_bf16(h)
    w_hi, w_lo = _split_bf16(wr)
    acc = jnp.dot(h_hi, w_hi, preferred_element_type=F32)
    acc = acc + jnp.dot(h_hi, w_lo, preferred_element_type=F32)
    acc = acc + jnp.dot(h_lo, w_hi, preferred_element_type=F32)
    return acc + br


def _store_slabs(ref, val):
    tm = val.shape[0]
    for s in range(SLABS):
        ref[pl.ds(s, tm, stride=SLABS), :] = val[:, s * LANES:(s + 1) * LANES]


def _load_slabs(ref, base, tm):
    return jnp.concatenate([ref[pl.ds(base + s, tm, stride=SLABS), :] for s in range(SLABS)], axis=1)


PROJ_TM = 256


def _proj_kernel(*refs, merge):
    if merge:
        (o0, o1, o2, l0, l1, l2, w_ref, x_ref, gate_ref, g_ref, b_ref, sc_ref, sh_ref, wr_ref, br_ref,
         xn_ref, hs_ref, lg_ref) = refs
        ls = [l0[...], l1[...], l2[...]]
        mx = jnp.maximum(jnp.maximum(ls[0], ls[1]), ls[2])
        es = [jnp.exp(l - mx) for l in ls]
        inv = 1.0 / (es[0] + es[1] + es[2])
        ws = [e * inv for e in es]
        parts = []
        for h in range(HEADS):
            sl = slice(h * HEAD_DIM, (h + 1) * HEAD_DIM)
            acc = ws[0][:, h:h + 1] * o0[:, sl].astype(F32)
            acc = acc + ws[1][:, h:h + 1] * o1[:, sl].astype(F32)
            acc = acc + ws[2][:, h:h + 1] * o2[:, sl].astype(F32)
            parts.append(acc.astype(BF16))
        y_in = jnp.concatenate(parts, axis=1)
    else:
        (y_ref, w_ref, x_ref, gate_ref, g_ref, b_ref, sc_ref, sh_ref, wr_ref, br_ref,
         xn_ref, hs_ref, lg_ref) = refs
        y_in = y_ref[...]
    y = jnp.dot(y_in, w_ref[...], preferred_element_type=F32)
    xn = _deepnorm(x_ref[...], y, gate_ref[0], g_ref[...], b_ref[...])
    xn_ref[...] = xn
    h = xn * (1.0 + sc_ref[0]) + sh_ref[0]
    _store_slabs(hs_ref, h)
    lg_ref[...] = _router_logits(h, wr_ref[...], br_ref[...])


def _proj_ln_router(mix_in, w_bf, x2, gate, ln_g, ln_b, scale, shift, wr, br, *, merge):
    tm = PROJ_TM
    per_b = SEQ // tm
    row = lambda i: (i, 0)
    full = lambda i: (0, 0)
    per_batch = lambda i: (i // per_b, 0, 0)
    if merge:
        mix_specs = [pl.BlockSpec((tm, D_MODEL), row)] * 3 + [pl.BlockSpec((tm, LANES), row)] * 3
    else:
        mix_specs = [pl.BlockSpec((tm, D_MODEL), row)]
    return pl.pallas_call(
        functools.partial(_proj_kernel, merge=merge),
        grid=(TOKENS // tm,),
        in_specs=mix_specs + [
            pl.BlockSpec((D_MODEL, D_MODEL), full),
            pl.BlockSpec((tm, D_MODEL), row),
            pl.BlockSpec((1, 1, D_MODEL), per_batch),
            pl.BlockSpec((1, D_MODEL), full),
            pl.BlockSpec((1, D_MODEL), full),
            pl.BlockSpec((1, 1, D_MODEL), per_batch),
            pl.BlockSpec((1, 1, D_MODEL), per_batch),
            pl.BlockSpec((D_MODEL, LANES), full),
            pl.BlockSpec((1, LANES), full)],
        out_specs=[pl.BlockSpec((tm, D_MODEL), row),
                   pl.BlockSpec((tm * SLABS, LANES), row),
                   pl.BlockSpec((tm, LANES), row)],
        out_shape=[jax.ShapeDtypeStruct((TOKENS, D_MODEL), F32),
                   jax.ShapeDtypeStruct((TOKENS * SLABS, LANES), F32),
                   jax.ShapeDtypeStruct((TOKENS, LANES), F32)],
        compiler_params=_cparams(("arbitrary",)),
        name="proj_merge" if merge else "proj",
    )(*mix_in, w_bf, x2, gate, ln_g, ln_b, scale, shift, wr, br)


DISP_TM = 256


def _slab_rows(row):
    if isinstance(row, int):
        return pl.ds(row * SLABS, SLABS)
    return pl.ds(pl.multiple_of(row * SLABS, SLABS), SLABS)


def _row_copy(src, dst, s_row, d_row, sem):
    return pltpu.make_async_copy(src.at[_slab_rows(s_row)], dst.at[_slab_rows(d_row)], sem)


def _dispatch_kernel(dest_ref, hs_hbm, xs_in, xs_hbm, sem):
    del xs_in
    i = pl.program_id(0)

    def issue(t, carry):
        tok = i * DISP_TM + t
        _row_copy(hs_hbm, xs_hbm, tok, dest_ref[2 * tok], sem).start()
        _row_copy(hs_hbm, xs_hbm, tok, dest_ref[2 * tok + 1], sem).start()
        return carry

    lax.fori_loop(0, DISP_TM, issue, 0)

    def drain(t, carry):
        _row_copy(hs_hbm, xs_hbm, 0, 0, sem).wait()
        _row_copy(hs_hbm, xs_hbm, 0, 0, sem).wait()
        return carry

    lax.fori_loop(0, DISP_TM, drain, 0)


def _dispatch(dest, hs, xs_zero):
    return pl.pallas_call(
        _dispatch_kernel,
        grid_spec=pltpu.PrefetchScalarGridSpec(
            num_scalar_prefetch=1,
            grid=(TOKENS // DISP_TM,),
            in_specs=[pl.BlockSpec(memory_space=pl.ANY), pl.BlockSpec(memory_space=pl.ANY)],
            out_specs=pl.BlockSpec(memory_space=pl.ANY),
            scratch_shapes=[pltpu.SemaphoreType.DMA(())]),
        out_shape=jax.ShapeDtypeStruct((MOE_ROWS * SLABS, LANES), F32),
        input_output_aliases={2: 0},
        compiler_params=pltpu.CompilerParams(dimension_semantics=("arbitrary",), has_side_effects=True),
        name="moe_dispatch",
    )(dest, hs, xs_zero)


def _moe_kernel(be_ref, nv_ref, xs_ref, wg_ref, wu_ref, wd_ref, ys_ref, wg_bf, wu_bf, wd_bf):
    i = pl.program_id(0)
    nv = nv_ref[i]

    @pl.when(nv > 0)
    def _():
        prev = be_ref[jnp.maximum(i - 1, 0)]

        @pl.when((i == 0) | (be_ref[i] != prev))
        def _():
            wg_bf[...] = wg_ref[0].astype(BF16)
            wu_bf[...] = wu_ref[0].astype(BF16)
            wd_bf[...] = wd_ref[0].astype(BF16)

        x = _load_slabs(xs_ref, 0, MOE_BLK)
        rows = lax.broadcasted_iota(jnp.int32, (MOE_BLK, 1), 0)
        x = jnp.where(rows < nv, x, 0.0).astype(BF16)
        g = jnp.dot(x, wg_bf[...], preferred_element_type=F32)
        u = jnp.dot(x, wu_bf[...], preferred_element_type=F32)
        a = (g * (1.0 / (1.0 + jnp.exp(-g)))) * u
        y = jnp.dot(a.astype(BF16), wd_bf[...], preferred_element_type=F32)
        _store_slabs(ys_ref, y)

    @pl.when(nv <= 0)
    def _():
        ys_ref[...] = jnp.zeros_like(ys_ref)


def _moe_ffn(block_e, block_nv, xs, w_gate, w_up, w_down):
    return pl.pallas_call(
        _moe_kernel,
        grid_spec=pltpu.PrefetchScalarGridSpec(
            num_scalar_prefetch=2,
            grid=(MOE_NB,),
            in_specs=[pl.BlockSpec((MOE_BLK * SLABS, LANES), lambda i, be, nv: (i, 0)),
                      pl.BlockSpec((1, D_MODEL, EXPERT_DIM), lambda i, be, nv: (be[i], 0, 0)),
                      pl.BlockSpec((1, D_MODEL, EXPERT_DIM), lambda i, be, nv: (be[i], 0, 0)),
                      pl.BlockSpec((1, EXPERT_DIM, D_MODEL), lambda i, be, nv: (be[i], 0, 0))],
            out_specs=pl.BlockSpec((MOE_BLK * SLABS, LANES), lambda i, be, nv: (i, 0)),
            scratch_shapes=[pltpu.VMEM((D_MODEL, EXPERT_DIM), BF16),
                            pltpu.VMEM((D_MODEL, EXPERT_DIM), BF16),
                            pltpu.VMEM((EXPERT_DIM, D_MODEL), BF16)]),
        out_shape=jax.ShapeDtypeStruct((MOE_ROWS * SLABS, LANES), F32),
        compiler_params=_cparams(("arbitrary",)),
        name="moe_ffn",
    )(block_e, block_nv, xs, w_gate, w_up, w_down)


COMB_TM = 256


def _combine_kernel(dest_ref, ys_hbm, gt_ref, x_ref, gate_ref, g_ref, b_ref, *rest, modulate):
    if modulate:
        sc_ref, sh_ref, xn_ref, hn_ref, buf, sem = rest
    else:
        xn_ref, buf, sem = rest
    i = pl.program_id(0)
    tm = COMB_TM

    def copy(src_row, slot):
        return _row_copy(ys_hbm, buf, src_row, slot, sem)

    def issue(t, carry):
        tok = i * tm + t
        copy(dest_ref[2 * tok], t).start()
        copy(dest_ref[2 * tok + 1], tm + t).start()
        return carry

    lax.fori_loop(0, tm, issue, 0)

    def drain(t, carry):
        copy(0, 0).wait()
        copy(0, 0).wait()
        return carry

    lax.fori_loop(0, tm, drain, 0)

    gt = gt_ref[...]
    y = _load_slabs(buf, 0, tm) * gt[:, 0:1] + _load_slabs(buf, tm * SLABS, tm) * gt[:, 1:2]
    xn = _deepnorm(x_ref[...], y, gate_ref[0], g_ref[...], b_ref[...])
    xn_ref[...] = xn
    if modulate:
        hn_ref[...] = (xn * (1.0 + sc_ref[0]) + sh_ref[0]).astype(BF16)


def _combine(dest, ys, gates, x2, gate, ln_g, ln_b, scale=None, shift=None):
    tm = COMB_TM
    per_b = SEQ // tm
    modulate = scale is not None
    row = lambda i, d: (i, 0)
    full = lambda i, d: (0, 0)
    per_batch = lambda i, d: (i // per_b, 0, 0)
    in_specs = [pl.BlockSpec(memory_space=pl.ANY),
                pl.BlockSpec((tm, 2), row),
                pl.BlockSpec((tm, D_MODEL), row),
                pl.BlockSpec((1, 1, D_MODEL), per_batch),
                pl.BlockSpec((1, D_MODEL), full),
                pl.BlockSpec((1, D_MODEL), full)]
    out_specs = [pl.BlockSpec((tm, D_MODEL), row)]
    out_shape = [jax.ShapeDtypeStruct((TOKENS, D_MODEL), F32)]
    args = [dest, ys, gates, x2, gate, ln_g, ln_b]
    if modulate:
        in_specs += [pl.BlockSpec((1, 1, D_MODEL), per_batch)] * 2
        out_specs.append(pl.BlockSpec((tm, D_MODEL), row))
        out_shape.append(jax.ShapeDtypeStruct((TOKENS, D_MODEL), BF16))
        args += [scale, shift]
    return pl.pallas_call(
        functools.partial(_combine_kernel, modulate=modulate),
        grid_spec=pltpu.PrefetchScalarGridSpec(
            num_scalar_prefetch=1,
            grid=(TOKENS // tm,),
            in_specs=in_specs,
            out_specs=out_specs,
            scratch_shapes=[pltpu.VMEM((2 * tm * SLABS, LANES), F32), pltpu.SemaphoreType.DMA(())]),
        out_shape=out_shape,
        compiler_params=_cparams(("arbitrary",)),
        name="moe_combine",
    )(*args)


def _route(logits):
    lc = logits[:, :N_GROUPS]
    lf = logits[:, N_GROUPS:N_GROUPS + N_EXPERTS].reshape(TOKENS, N_GROUPS, EPG)
    pg, gi = lax.top_k(jax.nn.softmax(lc, axis=-1), 1)
    lf_sel = jnp.take_along_axis(lf, gi[:, :, None], axis=1)[:, 0]
    pk, ek = lax.top_k(jax.nn.softmax(lf_sel, axis=-1), 2)
    gates = pg * (pk / jnp.sum(pk, axis=-1, keepdims=True))
    e_flat = (gi * EPG + ek).reshape(-1).astype(jnp.int32)
    onehot = (e_flat[:, None] == jnp.arange(N_EXPERTS, dtype=jnp.int32)[None, :]).astype(jnp.int32)
    csum = jnp.cumsum(onehot, axis=0)
    rank = jnp.sum((csum - onehot) * onehot, axis=1)
    counts = csum[-1]
    padded = (counts + MOE_BLK - 1) // MOE_BLK * MOE_BLK
    pends = jnp.cumsum(padded)
    pstarts = pends - padded
    dest = (pstarts[e_flat] + rank).astype(jnp.int32)
    blk_start = jnp.arange(MOE_NB, dtype=jnp.int32) * MOE_BLK
    block_e = jnp.clip(jnp.searchsorted(pends, blk_start, side='right'), 0, N_EXPERTS - 1).astype(jnp.int32)
    block_nv = jnp.clip(counts[block_e] - (blk_start - pstarts[block_e]), 0, MOE_BLK)
    block_nv = jnp.where(blk_start < pends[-1], block_nv, 0).astype(jnp.int32)
    return gates, dest, block_e, block_nv


def _moe(hs, logits, x2, gate, ln_g, ln_b, w_gate, w_up, w_down, scale=None, shift=None):
    gates, dest, block_e, block_nv = _route(logits)
    xs = _dispatch(dest, hs, jnp.zeros((MOE_ROWS * SLABS, LANES), F32))
    ys = _moe_ffn(block_e, block_nv, xs, w_gate, w_up, w_down)
    return _combine(dest, ys, gates, x2, gate, ln_g, ln_b, scale, shift)


FN_TM = 256


def _fnet_in_kernel(h_ref, w_ref, cs_ref, ab_ref):
    u = jnp.dot(h_ref[...], w_ref[...], preferred_element_type=F32).astype(BF16)
    for g in range(FNET_GROUPS):
        sl = slice(g * FNET_CH, (g + 1) * FNET_CH)
        ab = jnp.dot(u[:, sl], cs_ref[...], preferred_element_type=F32)
        ab_ref[0, 0, :, sl] = ab[:, :FNET_CH].astype(BF16)
        ab_ref[0, 1, :, sl] = ab[:, FNET_CH:].astype(BF16)


def _fnet_in(h_bf, w_in_bf, cs):
    tm = FN_TM
    per_b = SEQ // tm
    return pl.pallas_call(
        _fnet_in_kernel,
        grid=(TOKENS // tm,),
        in_specs=[pl.BlockSpec((tm, D_MODEL), lambda i: (i, 0)),
                  pl.BlockSpec((D_MODEL, D_MODEL), lambda i: (0, 0)),
                  pl.BlockSpec((FNET_CH, 2 * FNET_CH), lambda i: (0, 0))],
        out_specs=pl.BlockSpec((1, 2, tm, D_MODEL), lambda i: (i // per_b, 0, i % per_b, 0)),
        out_shape=jax.ShapeDtypeStruct((BATCH, 2, SEQ, D_MODEL), BF16),
        compiler_params=_cparams(("arbitrary",)),
        name="fnet_in",
    )(h_bf, w_in_bf, cs)


DFT_TM = 512
DFT_TK = 2048


def _dft_kernel(f_ref, ab_ref, y_ref, acc_ref):
    k = pl.program_id(2)

    @pl.when(k == 0)
    def _():
        acc_ref[...] = jnp.zeros_like(acc_ref)

    acc_ref[...] += jnp.dot(f_ref[...], ab_ref[0], preferred_element_type=F32)

    @pl.when(k == pl.num_programs(2) - 1)
    def _():
        y_ref[0] = acc_ref[...].astype(BF16)


def _seq_dft(f_cat, ab):
    tm, tk = DFT_TM, DFT_TK
    return pl.pallas_call(
        _dft_kernel,
        grid=(SEQ // tm, BATCH, 2 * SEQ // tk),
        in_specs=[pl.BlockSpec((tm, tk), lambda i, b, k: (i, k)),
                  pl.BlockSpec((1, tk, D_MODEL), lambda i, b, k: (b, k, 0))],
        out_specs=pl.BlockSpec((1, tm, D_MODEL), lambda i, b, k: (b, i, 0)),
        out_shape=jax.ShapeDtypeStruct((BATCH, SEQ, D_MODEL), BF16),
        scratch_shapes=[pltpu.VMEM((tm, D_MODEL), F32)],
        compiler_params=_cparams(("arbitrary", "arbitrary", "arbitrary")),
        name="seq_dft",
    )(f_cat, ab)


def _dft_constants():
    n = np.arange(FNET_CH)
    ang = 2.0 * np.pi * ((n[:, None] * n[None, :]) % FNET_CH) / FNET_CH
    cs = np.concatenate([np.cos(ang), np.sin(ang)], axis=1) / math.sqrt(FNET_CH)
    return jnp.asarray(cs, dtype=BF16)


def _seq_dft_matrix():
    k = lax.broadcasted_iota(jnp.int32, (SEQ, SEQ), 0)
    s = lax.broadcasted_iota(jnp.int32, (SEQ, SEQ), 1)
    ang = ((k * s) & (SEQ - 1)).astype(F32) * (2.0 * math.pi / SEQ)
    norm = 1.0 / math.sqrt(SEQ)
    return jnp.concatenate([(jnp.cos(ang) * norm).astype(BF16), (jnp.sin(ang) * (-norm)).astype(BF16)], axis=1)


def kernel(x, c, positions, ada_w, ada_b, attn_w_qkv, attn_w_o, fnet_w_in, fnet_w_out, ln_g, ln_b,
           router_coarse_w, router_coarse_b, router_fine_w, router_fine_b,
           expert_w_gate, expert_w_up, expert_w_down):
    x2 = x.reshape(TOKENS, D_MODEL)

    c8 = jnp.zeros((8, D_MODEL), F32).at[:BATCH].set(c)
    mod = _adaln(c8, ada_w.reshape(4, D_MODEL, 3 * D_MODEL), ada_b.reshape(4, 1, 3 * D_MODEL))[:, :BATCH]

    def mod_rows(idx):
        m = mod[idx].reshape(BATCH, 3, 1, D_MODEL)
        return m[:, 0], m[:, 1], m[:, 2]

    def router_params(i):
        wr = jnp.zeros((D_MODEL, LANES), F32)
        wr = wr.at[:, :N_GROUPS].set(router_coarse_w[i]).at[:, N_GROUPS:N_GROUPS + N_EXPERTS].set(router_fine_w[i])
        br = jnp.zeros((1, LANES), F32)
        br = br.at[0, :N_GROUPS].set(router_coarse_b[i]).at[0, N_GROUPS:N_GROUPS + N_EXPERTS].set(router_fine_b[i])
        return wr, br

    inv_freq = np.float32(ROPE_THETA) ** (-np.arange(0, ROT_DIM, 2, dtype=np.float32) / np.float32(ROT_DIM))
    invf = np.zeros((1, LANES), np.float32)
    invf[0, :ROT_HALF] = inv_freq
    invf[0, ROT_HALF:ROT_DIM] = inv_freq
    tabs = _rot_tables(positions.reshape(TOKENS, 1), jnp.asarray(invf))

    shift, scale, gate = mod_rows(0)
    qkv = _qkv(x2, scale, shift, attn_w_qkv[0], tabs).reshape(BATCH, SEQ, QKV_COLS)
    outs, lses = [], []
    for g, dil in enumerate(DILATIONS):
        o_g, lse_g = _attention_group(qkv, g, dil)
        outs.append(o_g)
        lses.append(lse_g)
    shift2, scale2, gate2 = mod_rows(1)
    wr, br = router_params(0)
    x2, hs, logits = _proj_ln_router(outs + lses, attn_w_o[0].astype(BF16), x2, gate, ln_g[0, 0][None], ln_b[0, 0][None],
                                     scale2, shift2, wr, br, merge=True)
    shift3, scale3, gate3 = mod_rows(2)
    x2, h_bf = _moe(hs, logits, x2, gate2, ln_g[0, 1][None], ln_b[0, 1][None],
                    expert_w_gate[0], expert_w_up[0], expert_w_down[0], scale3, shift3)

    ab = _fnet_in(h_bf, fnet_w_in[0].astype(BF16), _dft_constants()).reshape(BATCH, 2 * SEQ, D_MODEL)
    y = _seq_dft(_seq_dft_matrix(), ab).reshape(TOKENS, D_MODEL)
    shift4, scale4, gate4 = mod_rows(3)
    wr, br = router_params(1)
    x2, hs, logits = _proj_ln_router([y], fnet_w_out[0].astype(BF16), x2, gate3, ln_g[1, 0][None], ln_b[1, 0][None],
                                     scale4, shift4, wr, br, merge=False)
    (x2,) = _moe(hs, logits, x2, gate4, ln_g[1, 1][None], ln_b[1, 1][None],
                 expert_w_gate[1], expert_w_up[1], expert_w_down[1])
    return x2.reshape(BATCH, SEQ, D_MODEL)
```

```python
import functools
import math

import numpy as np
import jax
import jax.numpy as jnp
from jax import lax
from jax.experimental import pallas as pl
from jax.experimental.pallas import tpu as pltpu

F32 = jnp.float32
BF16 = jnp.bfloat16

D_MODEL = 2048
BATCH = 2
SEQ = 4096
TOKENS = BATCH * SEQ
DEPTH = 2
HEAD_DIM = 128
HEADS = D_MODEL // HEAD_DIM
DILATIONS = (1, 4, 16)
RADIUS = 64
N_DIL = 3
QKV_COLS = N_DIL * 3 * D_MODEL
ROT_DIM = HEAD_DIM // 4
ROT_HALF = ROT_DIM // 2
ROPE_THETA = 500000.0
FNET_GROUPS = 4
FNET_CH = D_MODEL // FNET_GROUPS
N_GROUPS = 4
EPG = 8
N_EXPERTS = N_GROUPS * EPG
EXPERT_DIM = D_MODEL // 4
LN_EPS = 1e-5
NEG_INF = -1e30
ALPHA = (2 * DEPTH) ** 0.25

LANES = 128
SLABS = D_MODEL // LANES
VMEM_LIMIT = 56 * 1024 * 1024

MOE_BLK = 256
MOE_NB = (2 * TOKENS) // MOE_BLK + N_EXPERTS
MOE_ROWS = MOE_NB * MOE_BLK


def _cparams(sem):
    return pltpu.CompilerParams(dimension_semantics=sem, vmem_limit_bytes=VMEM_LIMIT)


def _adaln_kernel(c_ref, w_ref, b_ref, o_ref):
    c = c_ref[...]
    sc = (c * (1.0 / (1.0 + jnp.exp(-c)))).astype(BF16)
    m = jnp.dot(sc, w_ref[0].astype(BF16), preferred_element_type=F32)
    o_ref[0] = m + b_ref[0]


def _adaln(c8, w4, b4):
    tn = 768
    n = 3 * D_MODEL
    return pl.pallas_call(
        _adaln_kernel,
        grid=(4, n // tn),
        in_specs=[pl.BlockSpec((8, D_MODEL), lambda s, j: (0, 0)),
                  pl.BlockSpec((1, D_MODEL, tn), lambda s, j: (s, 0, j)),
                  pl.BlockSpec((1, 1, tn), lambda s, j: (s, 0, j))],
        out_specs=pl.BlockSpec((1, 8, tn), lambda s, j: (s, 0, j)),
        out_shape=jax.ShapeDtypeStruct((4, 8, n), F32),
        compiler_params=_cparams(("arbitrary", "arbitrary")),
        name="adaln",
    )(c8, w4, b4)


def _rot_tab_kernel(pos_ref, invf_ref, o_ref):
    ang = pos_ref[...].astype(F32) * invf_ref[...]
    lane = lax.broadcasted_iota(jnp.int32, ang.shape, 1)
    c = jnp.where(lane < ROT_DIM, jnp.cos(ang), 1.0)
    s = jnp.sin(ang)
    s_lo = jnp.where(lane < ROT_HALF, -s, 0.0)
    s_hi = jnp.where((lane >= ROT_HALF) & (lane < ROT_DIM), s, 0.0)
    qs = HEAD_DIM ** -0.5
    o_ref[0] = c * qs
    o_ref[1] = s_lo * qs
    o_ref[2] = s_hi * qs
    o_ref[3] = c
    o_ref[4] = s_lo
    o_ref[5] = s_hi


def _rot_tables(pos_col, invf):
    tm = 1024
    return pl.pallas_call(
        _rot_tab_kernel,
        grid=(TOKENS // tm,),
        in_specs=[pl.BlockSpec((tm, 1), lambda i: (i, 0)),
                  pl.BlockSpec((1, LANES), lambda i: (0, 0))],
        out_specs=pl.BlockSpec((6, tm, LANES), lambda i: (0, i, 0)),
        out_shape=jax.ShapeDtypeStruct((6, TOKENS, LANES), F32),
        compiler_params=_cparams(("arbitrary",)),
        name="rot_tables",
    )(pos_col, invf)


QKV_TM = 1024
QKV_TN = 512


def _rotate(t, c, s_lo, s_hi):
    return (t * c + pltpu.roll(t, LANES - ROT_HALF, axis=1) * s_lo
            + pltpu.roll(t, ROT_HALF, axis=1) * s_hi)


def _qkv_kernel(x_ref, sc_ref, sh_ref, w_ref, tab_ref, o_ref, h_ref, r_ref, *, dil):
    j = pl.program_id(1)

    @pl.when(j == 0)
    def _():
        h_ref[...] = (x_ref[...] * (1.0 + sc_ref[0]) + sh_ref[0]).astype(BF16)

    acc = jnp.dot(h_ref[...], w_ref[0].astype(BF16), preferred_element_type=F32)
    which = j // (D_MODEL // QKV_TN)

    chunks = [slice(hc * LANES, (hc + 1) * LANES) for hc in range(QKV_TN // LANES)]

    def rot(base):
        for hc, sl in enumerate(chunks):
            r_ref[hc] = _rotate(acc[:, sl], tab_ref[base], tab_ref[base + 1], tab_ref[base + 2])

    @pl.when(which == 0)
    def _():
        rot(0)

    @pl.when(which == 1)
    def _():
        rot(3)

    @pl.when(which == 2)
    def _():
        for hc, sl in enumerate(chunks):
            r_ref[hc] = acc[:, sl]

    rows = QKV_TM // dil
    for r in range(dil):
        for hc, sl in enumerate(chunks):
            o_ref[0, r, :, sl] = r_ref[hc, pl.ds(r, rows, stride=dil), :].astype(BF16)


def _qkv(x2, scale, shift, w_qkv, tabs, g, dil):
    tm, tn = QKV_TM, QKV_TN
    per_b = SEQ // tm
    n_j = 3 * D_MODEL // tn
    return pl.pallas_call(
        functools.partial(_qkv_kernel, dil=dil),
        grid=(TOKENS // tm, n_j),
        in_specs=[pl.BlockSpec((tm, D_MODEL), lambda i, j: (i, 0)),
                  pl.BlockSpec((1, 1, D_MODEL), lambda i, j: (i // per_b, 0, 0)),
                  pl.BlockSpec((1, 1, D_MODEL), lambda i, j: (i // per_b, 0, 0)),
                  pl.BlockSpec((1, D_MODEL, tn), lambda i, j: (0, 0, g * n_j + j)),
                  pl.BlockSpec((6, tm, LANES), lambda i, j: (0, i, 0))],
        out_specs=pl.BlockSpec((1, dil, tm // dil, tn), lambda i, j: (i // per_b, 0, i % per_b, j)),
        out_shape=jax.ShapeDtypeStruct((BATCH, dil, SEQ // dil, 3 * D_MODEL), BF16),
        scratch_shapes=[pltpu.VMEM((tm, D_MODEL), BF16), pltpu.VMEM((tn // LANES, tm, LANES), F32)],
        compiler_params=_cparams(("arbitrary", "arbitrary")),
        name=f"qkv_proj_dil{dil}",
    )(x2, scale, shift, w_qkv, tabs)


ATT_TQ = 128
ATT_TK = ATT_TQ + 2 * RADIUS


def _attn_kernel(q_ref, kp_ref, kc_ref, kn_ref, vp_ref, vc_ref, vn_ref, o_ref, lse_ref, kbuf, vbuf, *, seq_len):
    i = pl.program_id(2)
    kbuf[0:RADIUS] = kp_ref[0, 0]
    kbuf[RADIUS:RADIUS + ATT_TQ] = kc_ref[0, 0]
    kbuf[RADIUS + ATT_TQ:ATT_TK] = kn_ref[0, 0]
    vbuf[0:RADIUS] = vp_ref[0, 0]
    vbuf[RADIUS:RADIUS + ATT_TQ] = vc_ref[0, 0]
    vbuf[RADIUS + ATT_TQ:ATT_TK] = vn_ref[0, 0]

    qpos = i * ATT_TQ + lax.broadcasted_iota(jnp.int32, (ATT_TQ, ATT_TK), 0)
    kpos = i * ATT_TQ - RADIUS + lax.broadcasted_iota(jnp.int32, (ATT_TQ, ATT_TK), 1)
    valid = (jnp.abs(qpos - kpos) <= RADIUS) & (kpos >= 0) & (kpos < seq_len)
    bias = jnp.where(valid, 0.0, NEG_INF).astype(F32)
    lane = lax.broadcasted_iota(jnp.int32, (ATT_TQ, LANES), 1)
    lse_all = jnp.zeros((ATT_TQ, LANES), F32)
    for h in range(HEADS):
        sl = slice(h * HEAD_DIM, (h + 1) * HEAD_DIM)
        s = lax.dot_general(q_ref[0, 0, :, sl], kbuf[:, sl], (((1,), (1,)), ((), ())),
                            preferred_element_type=F32) + bias
        m = jnp.max(s, axis=-1, keepdims=True)
        p = jnp.exp(s - m)
        den = jnp.sum(p, axis=-1, keepdims=True)
        o = jnp.dot(p.astype(BF16), vbuf[:, sl], preferred_element_type=F32) / den
        o_ref[0, 0, :, sl] = o.astype(BF16)
        lse_all = jnp.where(lane == h, m + jnp.log(den), lse_all)
    lse_ref[0, 0] = lse_all


def _attention_group(qkv, dil):
    seq_len = SEQ // dil
    tq = ATT_TQ
    sub = tq // RADIUS
    n_sub = seq_len // RADIUS

    def cur(which):
        return pl.BlockSpec((1, 1, tq, D_MODEL), lambda b, r, i: (b, r, i, which))

    def prev(which):
        return pl.BlockSpec((1, 1, RADIUS, D_MODEL), lambda b, r, i: (b, r, jnp.maximum(i * sub - 1, 0), which))

    def nxt(which):
        return pl.BlockSpec((1, 1, RADIUS, D_MODEL),
                            lambda b, r, i: (b, r, jnp.minimum((i + 1) * sub, n_sub - 1), which))

    return pl.pallas_call(
        functools.partial(_attn_kernel, seq_len=seq_len),
        grid=(BATCH, dil, seq_len // tq),
        in_specs=[cur(0), prev(1), cur(1), nxt(1), prev(2), cur(2), nxt(2)],
        out_specs=[pl.BlockSpec((1, 1, tq, D_MODEL), lambda b, r, i: (b, r, i, 0)),
                   pl.BlockSpec((1, 1, tq, LANES), lambda b, r, i: (b, r, i, 0))],
        out_shape=[jax.ShapeDtypeStruct((BATCH, dil, seq_len, D_MODEL), BF16),
                   jax.ShapeDtypeStruct((BATCH, dil, seq_len, LANES), F32)],
        scratch_shapes=[pltpu.VMEM((ATT_TK, D_MODEL), BF16), pltpu.VMEM((ATT_TK, D_MODEL), BF16)],
        compiler_params=_cparams(("arbitrary", "arbitrary", "arbitrary")),
        name=f"attn_dil{dil}",
    )(qkv, qkv, qkv, qkv, qkv, qkv, qkv)


def _deepnorm(x, y, gate, g, b):
    z = ALPHA * x + gate * y
    mu = jnp.mean(z, axis=-1, keepdims=True)
    zc = z - mu
    var = jnp.mean(zc * zc, axis=-1, keepdims=True)
    return zc * lax.rsqrt(var + LN_EPS) * g + b


def _split_bf16(a):
    hi = a.astype(BF16)
    lo = (a - hi.astype(F32)).astype(BF16)
    return hi, lo


def _router_logits(h, wr, br):
    h_hi, h_lo = _split_bf16(h)
    w_hi, w_lo = _split_bf16(wr)
    acc = jnp.dot(h_hi, w_hi, preferred_element_type=F32)
    acc = acc + jnp.dot(h_hi, w_lo, preferred_element_type=F32)
    acc = acc + jnp.dot(h_lo, w_hi, preferred_element_type=F32)
    return acc + br


def _store_slabs(ref, val):
    tm = val.shape[0]
    for s in range(SLABS):
        ref[pl.ds(s, tm, stride=SLABS), :] = val[:, s * LANES:(s + 1) * LANES]


def _load_slabs(ref, base, tm):
    return jnp.concatenate([ref[pl.ds(base + s, tm, stride=SLABS), :] for s in range(SLABS)], axis=1)


PROJ_TM = 256


def _proj_kernel(*refs, merge):
    if merge:
        (o0, o1, o2, l0, l1, l2, w_ref, x_ref, gate_ref, g_ref, b_ref, sc_ref, sh_ref, wr_ref, br_ref,
         xn_ref, hs_ref, lg_ref, on1, on2, ln1, ln2) = refs
        for src, dst, dil in ((o1, on1, DILATIONS[1]), (o2, on2, DILATIONS[2])):
            for r in range(dil):
                for h in range(HEADS):
                    dst[h, pl.ds(r, PROJ_TM // dil, stride=dil), :] = (
                        src[0, r, :, h * HEAD_DIM:(h + 1) * HEAD_DIM].astype(F32))
        for src, dst, dil in ((l1, ln1, DILATIONS[1]), (l2, ln2, DILATIONS[2])):
            for r in range(dil):
                dst[pl.ds(r, PROJ_TM // dil, stride=dil), :] = src[0, r]
        ls = [l0[0, 0], ln1[...], ln2[...]]
        mx = jnp.maximum(jnp.maximum(ls[0], ls[1]), ls[2])
        es = [jnp.exp(l - mx) for l in ls]
        inv = 1.0 / (es[0] + es[1] + es[2])
        ws = [e * inv for e in es]
        parts = []
        for h in range(HEADS):
            sl = slice(h * HEAD_DIM, (h + 1) * HEAD_DIM)
            acc = ws[0][:, h:h + 1] * o0[0, 0, :, sl].astype(F32)
            acc = acc + ws[1][:, h:h + 1] * on1[h]
            acc = acc + ws[2][:, h:h + 1] * on2[h]
            parts.append(acc.astype(BF16))
        y_in = jnp.concatenate(parts, axis=1)
    else:
        (y_ref, w_ref, x_ref, gate_ref, g_ref, b_ref, sc_ref, sh_ref, wr_ref, br_ref,
         xn_ref, hs_ref, lg_ref) = refs
        y_in = y_ref[...]
    y = jnp.dot(y_in, w_ref[...], preferred_element_type=F32)
    xn = _deepnorm(x_ref[...], y, gate_ref[0], g_ref[...], b_ref[...])
    xn_ref[...] = xn
    h = xn * (1.0 + sc_ref[0]) + sh_ref[0]
    _store_slabs(hs_ref, h)
    lg_ref[...] = _router_logits(h, wr_ref[...], br_ref[...])


def _proj_ln_router(mix_in, w_bf, x2, gate, ln_g, ln_b, scale, shift, wr, br, *, merge):
    tm = PROJ_TM
    per_b = SEQ // tm
    row = lambda i: (i, 0)
    full = lambda i: (0, 0)
    per_batch = lambda i: (i // per_b, 0, 0)
    scratch = []
    if merge:
        def resid(width, dil):
            return pl.BlockSpec((1, dil, tm // dil, width), lambda i: (i // per_b, 0, i % per_b, 0))
        mix_specs = ([resid(D_MODEL, dil) for dil in DILATIONS] + [resid(LANES, dil) for dil in DILATIONS])
        scratch = [pltpu.VMEM((HEADS, tm, HEAD_DIM), F32), pltpu.VMEM((HEADS, tm, HEAD_DIM), F32),
                   pltpu.VMEM((tm, LANES), F32), pltpu.VMEM((tm, LANES), F32)]
    else:
        mix_specs = [pl.BlockSpec((tm, D_MODEL), row)]
    return pl.pallas_call(
        functools.partial(_proj_kernel, merge=merge),
        grid=(TOKENS // tm,),
        in_specs=mix_specs + [
            pl.BlockSpec((D_MODEL, D_MODEL), full),
            pl.BlockSpec((tm, D_MODEL), row),
            pl.BlockSpec((1, 1, D_MODEL), per_batch),
            pl.BlockSpec((1, D_MODEL), full),
            pl.BlockSpec((1, D_MODEL), full),
            pl.BlockSpec((1, 1, D_MODEL), per_batch),
            pl.BlockSpec((1, 1, D_MODEL), per_batch),
            pl.BlockSpec((D_MODEL, LANES), full),
            pl.BlockSpec((1, LANES), full)],
        out_specs=[pl.BlockSpec((tm, D_MODEL), row),
                   pl.BlockSpec((tm * SLABS, LANES), row),
                   pl.BlockSpec((tm, LANES), row)],
        out_shape=[jax.ShapeDtypeStruct((TOKENS, D_MODEL), F32),
                   jax.ShapeDtypeStruct((TOKENS * SLABS, LANES), F32),
                   jax.ShapeDtypeStruct((TOKENS, LANES), F32)],
        scratch_shapes=scratch,
        compiler_params=_cparams(("arbitrary",)),
        name="proj_merge" if merge else "proj",
    )(*mix_in, w_bf, x2, gate, ln_g, ln_b, scale, shift, wr, br)


def _slab_rows(row):
    if isinstance(row, int):
        return pl.ds(row * SLABS, SLABS)
    return pl.ds(pl.multiple_of(row * SLABS, SLABS), SLABS)


def _row_copy(src, dst, s_row, d_row, sem):
    return pltpu.make_async_copy(src.at[_slab_rows(s_row)], dst.at[_slab_rows(d_row)], sem)


def _moe_kernel(be_ref, nv_ref, rt_ref, hs_hbm, wg_ref, wu_ref, wd_ref, ys_ref, xbuf, wg_bf, wu_bf, wd_bf, sems):
    i = pl.program_id(0)
    n_blk = pl.num_programs(0)
    nv = nv_ref[i]

    def gather(blk, slot):
        def issue(k, carry):
            _row_copy(hs_hbm, xbuf, rt_ref[blk * MOE_BLK + k], slot * MOE_BLK + k, sems.at[slot]).start()
            return carry
        lax.fori_loop(0, MOE_BLK, issue, 0, unroll=8)

    @pl.when(i == 0)
    def _():
        gather(0, 0)

    nxt = jnp.minimum(i + 1, n_blk - 1)

    @pl.when((i + 1 < n_blk) & (nv_ref[nxt] > 0))
    def _():
        gather(nxt, nxt % 2)

    @pl.when(nv > 0)
    def _():
        prev = be_ref[jnp.maximum(i - 1, 0)]

        @pl.when((i == 0) | (be_ref[i] != prev))
        def _():
            wg_bf[...] = wg_ref[0, 0].astype(BF16)
            wu_bf[...] = wu_ref[0, 0].astype(BF16)
            wd_bf[...] = wd_ref[0, 0].astype(BF16)

        slot = i % 2

        def drain(k, carry):
            _row_copy(hs_hbm, xbuf, 0, 0, sems.at[slot]).wait()
            return carry
        lax.fori_loop(0, MOE_BLK, drain, 0, unroll=8)

        x = _load_slabs(xbuf, pl.multiple_of(slot * (MOE_BLK * SLABS), MOE_BLK * SLABS), MOE_BLK)
        rows = lax.broadcasted_iota(jnp.int32, (MOE_BLK, 1), 0)
        x = jnp.where(rows < nv, x, 0.0).astype(BF16)
        g = jnp.dot(x, wg_bf[...], preferred_element_type=F32)
        u = jnp.dot(x, wu_bf[...], preferred_element_type=F32)
        a = (g * (1.0 / (1.0 + jnp.exp(-g)))) * u
        y = jnp.dot(a.astype(BF16), wd_bf[...], preferred_element_type=F32)
        _store_slabs(ys_ref, y)

    @pl.when(nv <= 0)
    def _():
        ys_ref[...] = jnp.zeros_like(ys_ref)


def _moe_ffn(block_e, block_nv, row_tok, hs, w_gate, w_up, w_down, layer):
    wmap = lambda i, be, nv, rt: (layer, be[i], 0, 0)
    return pl.pallas_call(
        _moe_kernel,
        grid_spec=pltpu.PrefetchScalarGridSpec(
            num_scalar_prefetch=3,
            grid=(MOE_NB,),
            in_specs=[pl.BlockSpec(memory_space=pl.ANY),
                      pl.BlockSpec((1, 1, D_MODEL, EXPERT_DIM), wmap),
                      pl.BlockSpec((1, 1, D_MODEL, EXPERT_DIM), wmap),
                      pl.BlockSpec((1, 1, EXPERT_DIM, D_MODEL), wmap)],
            out_specs=pl.BlockSpec((MOE_BLK * SLABS, LANES), lambda i, be, nv, rt: (i, 0)),
            scratch_shapes=[pltpu.VMEM((2 * MOE_BLK * SLABS, LANES), F32),
                            pltpu.VMEM((D_MODEL, EXPERT_DIM), BF16),
                            pltpu.VMEM((D_MODEL, EXPERT_DIM), BF16),
                            pltpu.VMEM((EXPERT_DIM, D_MODEL), BF16),
                            pltpu.SemaphoreType.DMA((2,))]),
        out_shape=jax.ShapeDtypeStruct((MOE_ROWS * SLABS, LANES), F32),
        compiler_params=pltpu.CompilerParams(dimension_semantics=("arbitrary",), vmem_limit_bytes=VMEM_LIMIT,
                                             disable_bounds_checks=True),
        name="moe_ffn",
    )(block_e, block_nv, row_tok, hs, w_gate, w_up, w_down)


COMB_TM = 256


def _combine_kernel(dest_ref, ys_hbm, gt_ref, x_ref, gate_ref, g_ref, b_ref, *rest, modulate):
    if modulate:
        sc_ref, sh_ref, xn_ref, hn_ref, buf, sem = rest
    else:
        xn_ref, buf, sem = rest
    i = pl.program_id(0)
    tm = COMB_TM

    def copy(src_row, slot):
        return _row_copy(ys_hbm, buf, src_row, slot, sem)

    def issue(t, carry):
        tok = i * tm + t
        copy(dest_ref[2 * tok], t).start()
        copy(dest_ref[2 * tok + 1], tm + t).start()
        return carry

    lax.fori_loop(0, tm, issue, 0)

    def drain(t, carry):
        copy(0, 0).wait()
        copy(0, 0).wait()
        return carry

    lax.fori_loop(0, tm, drain, 0)

    gt = gt_ref[...]
    y = _load_slabs(buf, 0, tm) * gt[:, 0:1] + _load_slabs(buf, tm * SLABS, tm) * gt[:, 1:2]
    xn = _deepnorm(x_ref[...], y, gate_ref[0], g_ref[...], b_ref[...])
    xn_ref[...] = xn
    if modulate:
        hn_ref[...] = (xn * (1.0 + sc_ref[0]) + sh_ref[0]).astype(BF16)


def _combine(dest, ys, gates, x2, gate, ln_g, ln_b, scale=None, shift=None):
    tm = COMB_TM
    per_b = SEQ // tm
    modulate = scale is not None
    row = lambda i, d: (i, 0)
    full = lambda i, d: (0, 0)
    per_batch = lambda i, d: (i // per_b, 0, 0)
    in_specs = [pl.BlockSpec(memory_space=pl.ANY),
                pl.BlockSpec((tm, 2), row),
                pl.BlockSpec((tm, D_MODEL), row),
                pl.BlockSpec((1, 1, D_MODEL), per_batch),
                pl.BlockSpec((1, D_MODEL), full),
                pl.BlockSpec((1, D_MODEL), full)]
    out_specs = [pl.BlockSpec((tm, D_MODEL), row)]
    out_shape = [jax.ShapeDtypeStruct((TOKENS, D_MODEL), F32)]
    args = [dest, ys, gates, x2, gate, ln_g, ln_b]
    if modulate:
        in_specs += [pl.BlockSpec((1, 1, D_MODEL), per_batch)] * 2
        out_specs.append(pl.BlockSpec((tm, D_MODEL), row))
        out_shape.append(jax.ShapeDtypeStruct((TOKENS, D_MODEL), BF16))
        args += [scale, shift]
    return pl.pallas_call(
        functools.partial(_combine_kernel, modulate=modulate),
        grid_spec=pltpu.PrefetchScalarGridSpec(
            num_scalar_prefetch=1,
            grid=(TOKENS // tm,),
            in_specs=in_specs,
            out_specs=out_specs,
            scratch_shapes=[pltpu.VMEM((2 * tm * SLABS, LANES), F32), pltpu.SemaphoreType.DMA(())]),
        out_shape=out_shape,
        compiler_params=_cparams(("arbitrary",)),
        name="moe_combine",
    )(*args)


def _route(logits):
    lc = logits[:, :N_GROUPS]
    lf = logits[:, N_GROUPS:N_GROUPS + N_EXPERTS].reshape(TOKENS, N_GROUPS, EPG)
    pc = jax.nn.softmax(lc, axis=-1)
    pg = jnp.max(pc, axis=-1, keepdims=True)
    gi = jnp.argmax(pc, axis=-1)[:, None]
    lf_sel = jnp.take_along_axis(lf, gi[:, :, None], axis=1)[:, 0]
    pf = jax.nn.softmax(lf_sel, axis=-1)
    e1 = jnp.argmax(pf, axis=-1)[:, None]
    p1 = jnp.max(pf, axis=-1, keepdims=True)
    pf2 = jnp.where(jnp.arange(EPG)[None, :] == e1, -1.0, pf)
    e2 = jnp.argmax(pf2, axis=-1)[:, None]
    p2 = jnp.max(pf2, axis=-1, keepdims=True)
    pk = jnp.concatenate([p1, p2], axis=1)
    ek = jnp.concatenate([e1, e2], axis=1)
    gates = pg * (pk / jnp.sum(pk, axis=-1, keepdims=True))
    e_flat = (gi * EPG + ek).reshape(-1).astype(jnp.int32)
    onehot = (e_flat[:, None] == jnp.arange(N_EXPERTS, dtype=jnp.int32)[None, :]).astype(jnp.int32)
    csum = jnp.cumsum(onehot, axis=0)
    rank = jnp.sum((csum - onehot) * onehot, axis=1)
    counts = csum[-1]
    padded = (counts + MOE_BLK - 1) // MOE_BLK * MOE_BLK
    pends = jnp.cumsum(padded)
    pstarts = pends - padded
    dest = (pstarts[e_flat] + rank).astype(jnp.int32)
    blk_start = jnp.arange(MOE_NB, dtype=jnp.int32) * MOE_BLK
    block_e = jnp.clip(jnp.searchsorted(pends, blk_start, side='right'), 0, N_EXPERTS - 1).astype(jnp.int32)
    block_nv = jnp.clip(counts[block_e] - (blk_start - pstarts[block_e]), 0, MOE_BLK)
    block_nv = jnp.where(blk_start < pends[-1], block_nv, 0).astype(jnp.int32)
    row_tok = jnp.zeros((MOE_ROWS,), jnp.int32).at[dest].set(jnp.arange(2 * TOKENS, dtype=jnp.int32) // 2)
    return gates, dest, row_tok, block_e, block_nv


def _moe(hs, logits, x2, gate, ln_g, ln_b, w_gate, w_up, w_down, layer, scale=None, shift=None):
    gates, dest, row_tok, block_e, block_nv = _route(logits)
    ys = _moe_ffn(block_e, block_nv, row_tok, hs, w_gate, w_up, w_down, layer)
    return _combine(dest, ys, gates, x2, gate, ln_g, ln_b, scale, shift)


FN_TM = 256


def _fnet_in_kernel(h_ref, w_ref, cs_ref, ab_ref):
    u = jnp.dot(h_ref[...], w_ref[...], preferred_element_type=F32).astype(BF16)
    for g in range(FNET_GROUPS):
        sl = slice(g * FNET_CH, (g + 1) * FNET_CH)
        ab = jnp.dot(u[:, sl], cs_ref[...], preferred_element_type=F32)
        ab_ref[0, 0, :, sl] = ab[:, :FNET_CH].astype(BF16)
        ab_ref[0, 1, :, sl] = ab[:, FNET_CH:].astype(BF16)


def _fnet_in(h_bf, w_in_bf, cs):
    tm = FN_TM
    per_b = SEQ // tm
    return pl.pallas_call(
        _fnet_in_kernel,
        grid=(TOKENS // tm,),
        in_specs=[pl.BlockSpec((tm, D_MODEL), lambda i: (i, 0)),
                  pl.BlockSpec((D_MODEL, D_MODEL), lambda i: (0, 0)),
                  pl.BlockSpec((FNET_CH, 2 * FNET_CH), lambda i: (0, 0))],
        out_specs=pl.BlockSpec((1, 2, tm, D_MODEL), lambda i: (i // per_b, 0, i % per_b, 0)),
        out_shape=jax.ShapeDtypeStruct((BATCH, 2, SEQ, D_MODEL), BF16),
        compiler_params=_cparams(("arbitrary",)),
        name="fnet_in",
    )(h_bf, w_in_bf, cs)


DFT_TM = 512
DFT_TK = 2048


def _dft_kernel(f_ref, ab_ref, y_ref, acc_ref):
    k = pl.program_id(2)

    @pl.when(k == 0)
    def _():
        acc_ref[...] = jnp.zeros_like(acc_ref)

    acc_ref[...] += jnp.dot(f_ref[...], ab_ref[0], preferred_element_type=F32)

    @pl.when(k == pl.num_programs(2) - 1)
    def _():
        y_ref[0] = acc_ref[...].astype(BF16)


def _seq_dft(f_cat, ab):
    tm, tk = DFT_TM, DFT_TK
    return pl.pallas_call(
        _dft_kernel,
        grid=(SEQ // tm, BATCH, 2 * SEQ // tk),
        in_specs=[pl.BlockSpec((tm, tk), lambda i, b, k: (i, k)),
                  pl.BlockSpec((1, tk, D_MODEL), lambda i, b, k: (b, k, 0))],
        out_specs=pl.BlockSpec((1, tm, D_MODEL), lambda i, b, k: (b, i, 0)),
        out_shape=jax.ShapeDtypeStruct((BATCH, SEQ, D_MODEL), BF16),
        scratch_shapes=[pltpu.VMEM((tm, D_MODEL), F32)],
        compiler_params=_cparams(("arbitrary", "arbitrary", "arbitrary")),
        name="seq_dft",
    )(f_cat, ab)


def _dft_constants():
    n = np.arange(FNET_CH)
    ang = 2.0 * np.pi * ((n[:, None] * n[None, :]) % FNET_CH) / FNET_CH
    cs = np.concatenate([np.cos(ang), np.sin(ang)], axis=1) / math.sqrt(FNET_CH)
    return jnp.asarray(cs, dtype=BF16)


def _seq_dft_matrix():
    k = lax.broadcasted_iota(jnp.int32, (SEQ, SEQ), 0)
    s = lax.broadcasted_iota(jnp.int32, (SEQ, SEQ), 1)
    ang = ((k * s) & (SEQ - 1)).astype(F32) * (2.0 * math.pi / SEQ)
    norm = 1.0 / math.sqrt(SEQ)
    return jnp.concatenate([(jnp.cos(ang) * norm).astype(BF16), (jnp.sin(ang) * (-norm)).astype(BF16)], axis=1)


def kernel(x, c, positions, ada_w, ada_b, attn_w_qkv, attn_w_o, fnet_w_in, fnet_w_out, ln_g, ln_b,
           router_coarse_w, router_coarse_b, router_fine_w, router_fine_b,
           expert_w_gate, expert_w_up, expert_w_down):
    x2 = x.reshape(TOKENS, D_MODEL)

    c8 = jnp.zeros((8, D_MODEL), F32).at[:BATCH].set(c)
    mod = _adaln(c8, ada_w.reshape(4, D_MODEL, 3 * D_MODEL), ada_b.reshape(4, 1, 3 * D_MODEL))[:, :BATCH]

    def mod_rows(idx):
        m = mod[idx].reshape(BATCH, 3, 1, D_MODEL)
        return m[:, 0], m[:, 1], m[:, 2]

    def router_params(i):
        wr = jnp.zeros((D_MODEL, LANES), F32)
        wr = wr.at[:, :N_GROUPS].set(router_coarse_w[i]).at[:, N_GROUPS:N_GROUPS + N_EXPERTS].set(router_fine_w[i])
        br = jnp.zeros((1, LANES), F32)
        br = br.at[0, :N_GROUPS].set(router_coarse_b[i]).at[0, N_GROUPS:N_GROUPS + N_EXPERTS].set(router_fine_b[i])
        return wr, br

    inv_freq = np.float32(ROPE_THETA) ** (-np.arange(0, ROT_DIM, 2, dtype=np.float32) / np.float32(ROT_DIM))
    invf = np.zeros((1, LANES), np.float32)
    invf[0, :ROT_HALF] = inv_freq
    invf[0, ROT_HALF:ROT_DIM] = inv_freq
    tabs = _rot_tables(positions.reshape(TOKENS, 1), jnp.asarray(invf))

    shift, scale, gate = mod_rows(0)
    outs, lses = [], []
    for g, dil in enumerate(DILATIONS):
        o_g, lse_g = _attention_group(_qkv(x2, scale, shift, attn_w_qkv, tabs, g, dil), dil)
        outs.append(o_g)
        lses.append(lse_g)
    shift2, scale2, gate2 = mod_rows(1)
    wr, br = router_params(0)
    x2, hs, logits = _proj_ln_router(outs + lses, attn_w_o[0].astype(BF16), x2, gate, ln_g[0, 0][None], ln_b[0, 0][None],
                                     scale2, shift2, wr, br, merge=True)
    shift3, scale3, gate3 = mod_rows(2)
    x2, h_bf = _moe(hs, logits, x2, gate2, ln_g[0, 1][None], ln_b[0, 1][None],
                    expert_w_gate, expert_w_up, expert_w_down, 0, scale3, shift3)

    ab = _fnet_in(h_bf, fnet_w_in[0].astype(BF16), _dft_constants()).reshape(BATCH, 2 * SEQ, D_MODEL)
    y = _seq_dft(_seq_dft_matrix(), ab).reshape(TOKENS, D_MODEL)
    shift4, scale4, gate4 = mod_rows(3)
    wr, br = router_params(1)
    x2, hs, logits = _proj_ln_router([y], fnet_w_out[0].astype(BF16), x2, gate3, ln_g[1, 0][None], ln_b[1, 0][None],
                                     scale4, shift4, wr, br, merge=False)
    (x2,) = _moe(hs, logits, x2, gate4, ln_g[1, 1][None], ln_b[1, 1][None],
                 expert_w_gate, expert_w_up, expert_w_down, 1)
    return x2.reshape(BATCH, SEQ, D_MODEL)
```

```python
import functools
import math

import numpy as np
import jax
import jax.numpy as jnp
from jax import lax
from jax.experimental import pallas as pl
from jax.experimental.pallas import tpu as pltpu

F32 = jnp.float32
BF16 = jnp.bfloat16
U32 = jnp.uint32

D_MODEL = 2048
BATCH = 2
SEQ = 4096
TOKENS = BATCH * SEQ
DEPTH = 2
HEAD_DIM = 128
HEADS = D_MODEL // HEAD_DIM
DILATIONS = (1, 4, 16)
RADIUS = 64
N_DIL = 3
QKV_COLS = N_DIL * 3 * D_MODEL
ROT_DIM = HEAD_DIM // 4
ROT_HALF = ROT_DIM // 2
ROPE_THETA = 500000.0
FNET_GROUPS = 4
FNET_CH = D_MODEL // FNET_GROUPS
N_GROUPS = 4
EPG = 8
N_EXPERTS = N_GROUPS * EPG
EXPERT_DIM = D_MODEL // 4
LN_EPS = 1e-5
NEG_INF = -1e30
ALPHA = (2 * DEPTH) ** 0.25

LANES = 128
SLABS = D_MODEL // (2 * LANES)
VMEM_LIMIT = 56 * 1024 * 1024

MOE_BLK = 256
MOE_NB = (2 * TOKENS) // MOE_BLK + N_EXPERTS
MOE_ROWS = MOE_NB * MOE_BLK


def _cparams(sem):
    return pltpu.CompilerParams(dimension_semantics=sem, vmem_limit_bytes=VMEM_LIMIT)


def _adaln_kernel(c_ref, w_ref, b_ref, o_ref):
    c = c_ref[...]
    sc = (c * (1.0 / (1.0 + jnp.exp(-c)))).astype(BF16)
    m = jnp.dot(sc, w_ref[0].astype(BF16), preferred_element_type=F32)
    o_ref[0] = m + b_ref[0]


def _adaln(c8, w4, b4):
    tn = 768
    n = 3 * D_MODEL
    return pl.pallas_call(
        _adaln_kernel,
        grid=(4, n // tn),
        in_specs=[pl.BlockSpec((8, D_MODEL), lambda s, j: (0, 0)),
                  pl.BlockSpec((1, D_MODEL, tn), lambda s, j: (s, 0, j)),
                  pl.BlockSpec((1, 1, tn), lambda s, j: (s, 0, j))],
        out_specs=pl.BlockSpec((1, 8, tn), lambda s, j: (s, 0, j)),
        out_shape=jax.ShapeDtypeStruct((4, 8, n), F32),
        compiler_params=_cparams(("arbitrary", "arbitrary")),
        name="adaln",
    )(c8, w4, b4)


def _rot_tab_kernel(pos_ref, invf_ref, o_ref):
    ang = pos_ref[...].astype(F32) * invf_ref[...]
    lane = lax.broadcasted_iota(jnp.int32, ang.shape, 1)
    c = jnp.where(lane < ROT_DIM, jnp.cos(ang), 1.0)
    s = jnp.sin(ang)
    s_lo = jnp.where(lane < ROT_HALF, -s, 0.0)
    s_hi = jnp.where((lane >= ROT_HALF) & (lane < ROT_DIM), s, 0.0)
    qs = HEAD_DIM ** -0.5
    o_ref[0] = c * qs
    o_ref[1] = s_lo * qs
    o_ref[2] = s_hi * qs
    o_ref[3] = c
    o_ref[4] = s_lo
    o_ref[5] = s_hi


def _rot_tables(pos_col, invf):
    tm = 1024
    return pl.pallas_call(
        _rot_tab_kernel,
        grid=(TOKENS // tm,),
        in_specs=[pl.BlockSpec((tm, 1), lambda i: (i, 0)),
                  pl.BlockSpec((1, LANES), lambda i: (0, 0))],
        out_specs=pl.BlockSpec((6, tm, LANES), lambda i: (0, i, 0)),
        out_shape=jax.ShapeDtypeStruct((6, TOKENS, LANES), F32),
        compiler_params=_cparams(("arbitrary",)),
        name="rot_tables",
    )(pos_col, invf)


QKV_TM = 1024
QKV_TN = 512


def _rotate(t, c, s_lo, s_hi):
    return (t * c + pltpu.roll(t, LANES - ROT_HALF, axis=1) * s_lo
            + pltpu.roll(t, ROT_HALF, axis=1) * s_hi)


def _qkv_kernel(x_ref, sc_ref, sh_ref, w_ref, tab_ref, o_ref, h_ref, r_ref, *, dil):
    j = pl.program_id(1)

    @pl.when(j == 0)
    def _():
        h_ref[...] = (x_ref[...] * (1.0 + sc_ref[0]) + sh_ref[0]).astype(BF16)

    acc = jnp.dot(h_ref[...], w_ref[0].astype(BF16), preferred_element_type=F32)
    which = j // (D_MODEL // QKV_TN)

    chunks = [slice(hc * LANES, (hc + 1) * LANES) for hc in range(QKV_TN // LANES)]

    def rot(base):
        for hc, sl in enumerate(chunks):
            r_ref[hc] = _rotate(acc[:, sl], tab_ref[base], tab_ref[base + 1], tab_ref[base + 2])

    @pl.when(which == 0)
    def _():
        rot(0)

    @pl.when(which == 1)
    def _():
        rot(3)

    @pl.when(which == 2)
    def _():
        for hc, sl in enumerate(chunks):
            r_ref[hc] = acc[:, sl]

    rows = QKV_TM // dil
    for r in range(dil):
        for hc, sl in enumerate(chunks):
            o_ref[0, r, :, sl] = r_ref[hc, pl.ds(r, rows, stride=dil), :].astype(BF16)


def _qkv(x2, scale, shift, w_qkv, tabs, g, dil):
    tm, tn = QKV_TM, QKV_TN
    per_b = SEQ // tm
    n_j = 3 * D_MODEL // tn
    return pl.pallas_call(
        functools.partial(_qkv_kernel, dil=dil),
        grid=(TOKENS // tm, n_j),
        in_specs=[pl.BlockSpec((tm, D_MODEL), lambda i, j: (i, 0)),
                  pl.BlockSpec((1, 1, D_MODEL), lambda i, j: (i // per_b, 0, 0)),
                  pl.BlockSpec((1, 1, D_MODEL), lambda i, j: (i // per_b, 0, 0)),
                  pl.BlockSpec((1, D_MODEL, tn), lambda i, j: (0, 0, g * n_j + j)),
                  pl.BlockSpec((6, tm, LANES), lambda i, j: (0, i, 0))],
        out_specs=pl.BlockSpec((1, dil, tm // dil, tn), lambda i, j: (i // per_b, 0, i % per_b, j)),
        out_shape=jax.ShapeDtypeStruct((BATCH, dil, SEQ // dil, 3 * D_MODEL), BF16),
        scratch_shapes=[pltpu.VMEM((tm, D_MODEL), BF16), pltpu.VMEM((tn // LANES, tm, LANES), F32)],
        compiler_params=_cparams(("arbitrary", "arbitrary")),
        name=f"qkv_proj_dil{dil}",
    )(x2, scale, shift, w_qkv, tabs)


ATT_TQ = 128
ATT_TK = ATT_TQ + 2 * RADIUS


def _attn_kernel(q_ref, kp_ref, kc_ref, kn_ref, vp_ref, vc_ref, vn_ref, o_ref, lse_ref, kbuf, vbuf, *, seq_len):
    i = pl.program_id(2)
    kbuf[0:RADIUS] = kp_ref[0, 0]
    kbuf[RADIUS:RADIUS + ATT_TQ] = kc_ref[0, 0]
    kbuf[RADIUS + ATT_TQ:ATT_TK] = kn_ref[0, 0]
    vbuf[0:RADIUS] = vp_ref[0, 0]
    vbuf[RADIUS:RADIUS + ATT_TQ] = vc_ref[0, 0]
    vbuf[RADIUS + ATT_TQ:ATT_TK] = vn_ref[0, 0]

    qpos = i * ATT_TQ + lax.broadcasted_iota(jnp.int32, (ATT_TQ, ATT_TK), 0)
    kpos = i * ATT_TQ - RADIUS + lax.broadcasted_iota(jnp.int32, (ATT_TQ, ATT_TK), 1)
    valid = (jnp.abs(qpos - kpos) <= RADIUS) & (kpos >= 0) & (kpos < seq_len)
    bias = jnp.where(valid, 0.0, NEG_INF).astype(F32)
    lane = lax.broadcasted_iota(jnp.int32, (ATT_TQ, LANES), 1)
    lse_all = jnp.zeros((ATT_TQ, LANES), F32)
    for h in range(HEADS):
        sl = slice(h * HEAD_DIM, (h + 1) * HEAD_DIM)
        s = lax.dot_general(q_ref[0, 0, :, sl], kbuf[:, sl], (((1,), (1,)), ((), ())),
                            preferred_element_type=F32) + bias
        m = jnp.max(s, axis=-1, keepdims=True)
        p = jnp.exp(s - m)
        den = jnp.sum(p, axis=-1, keepdims=True)
        o = jnp.dot(p.astype(BF16), vbuf[:, sl], preferred_element_type=F32) / den
        o_ref[0, 0, :, sl] = o.astype(BF16)
        lse_all = jnp.where(lane == h, m + jnp.log(den), lse_all)
    lse_ref[0, 0] = lse_all


def _attention_group(qkv, dil):
    seq_len = SEQ // dil
    tq = ATT_TQ
    sub = tq // RADIUS
    n_sub = seq_len // RADIUS

    def cur(which):
        return pl.BlockSpec((1, 1, tq, D_MODEL), lambda b, r, i: (b, r, i, which))

    def prev(which):
        return pl.BlockSpec((1, 1, RADIUS, D_MODEL), lambda b, r, i: (b, r, jnp.maximum(i * sub - 1, 0), which))

    def nxt(which):
        return pl.BlockSpec((1, 1, RADIUS, D_MODEL),
                            lambda b, r, i: (b, r, jnp.minimum((i + 1) * sub, n_sub - 1), which))

    return pl.pallas_call(
        functools.partial(_attn_kernel, seq_len=seq_len),
        grid=(BATCH, dil, seq_len // tq),
        in_specs=[cur(0), prev(1), cur(1), nxt(1), prev(2), cur(2), nxt(2)],
        out_specs=[pl.BlockSpec((1, 1, tq, D_MODEL), lambda b, r, i: (b, r, i, 0)),
                   pl.BlockSpec((1, 1, tq, LANES), lambda b, r, i: (b, r, i, 0))],
        out_shape=[jax.ShapeDtypeStruct((BATCH, dil, seq_len, D_MODEL), BF16),
                   jax.ShapeDtypeStruct((BATCH, dil, seq_len, LANES), F32)],
        scratch_shapes=[pltpu.VMEM((ATT_TK, D_MODEL), BF16), pltpu.VMEM((ATT_TK, D_MODEL), BF16)],
        compiler_params=_cparams(("arbitrary", "arbitrary", "arbitrary")),
        name=f"attn_dil{dil}",
    )(qkv, qkv, qkv, qkv, qkv, qkv, qkv)


def _deepnorm(x, y, gate, g, b):
    z = ALPHA * x + gate * y
    mu = jnp.mean(z, axis=-1, keepdims=True)
    zc = z - mu
    var = jnp.mean(zc * zc, axis=-1, keepdims=True)
    return zc * lax.rsqrt(var + LN_EPS) * g + b


def _split_bf16(a):
    hi = a.astype(BF16)
    lo = (a - hi.astype(F32)).astype(BF16)
    return hi, lo


def _router_logits(h, wr, br):
    h_hi, h_lo = _split_bf16(h)
    w_hi, w_lo = _split_bf16(wr)
    acc = jnp.dot(h_hi, w_hi, preferred_element_type=F32)
    acc = acc + jnp.dot(h_hi, w_lo, preferred_element_type=F32)
    acc = acc + jnp.dot(h_lo, w_hi, preferred_element_type=F32)
    return acc + br


def _store_slabs(ref, val):
    tm = val.shape[0]
    for s in range(SLABS):
        lo = val[:, (2 * s) * LANES:(2 * s + 1) * LANES].astype(BF16).astype(F32)
        hi = val[:, (2 * s + 1) * LANES:(2 * s + 2) * LANES].astype(BF16).astype(F32)
        word = ((lax.bitcast_convert_type(lo, U32) >> 16)
                | (lax.bitcast_convert_type(hi, U32) & jnp.uint32(0xFFFF0000)))
        ref[pl.ds(s, tm, stride=SLABS), :] = word


def _load_slabs(ref, base, tm):
    parts = []
    for s in range(SLABS):
        word = ref[pl.ds(base + s, tm, stride=SLABS), :]
        parts.append(lax.bitcast_convert_type(word << 16, F32))
        parts.append(lax.bitcast_convert_type(word & jnp.uint32(0xFFFF0000), F32))
    return jnp.concatenate(parts, axis=1)


PROJ_TM = 256


def _proj_kernel(*refs, merge):
    if merge:
        (o0, o1, o2, l0, l1, l2, w_ref, x_ref, gate_ref, g_ref, b_ref, sc_ref, sh_ref, wr_ref, br_ref,
         xn_ref, hs_ref, lg_ref, on1, on2, ln1, ln2) = refs
        for src, dst, dil in ((o1, on1, DILATIONS[1]), (o2, on2, DILATIONS[2])):
            for r in range(dil):
                for h in range(HEADS):
                    dst[h, pl.ds(r, PROJ_TM // dil, stride=dil), :] = (
                        src[0, r, :, h * HEAD_DIM:(h + 1) * HEAD_DIM].astype(F32))
        for src, dst, dil in ((l1, ln1, DILATIONS[1]), (l2, ln2, DILATIONS[2])):
            for r in range(dil):
                dst[pl.ds(r, PROJ_TM // dil, stride=dil), :] = src[0, r]
        ls = [l0[0, 0], ln1[...], ln2[...]]
        mx = jnp.maximum(jnp.maximum(ls[0], ls[1]), ls[2])
        es = [jnp.exp(l - mx) for l in ls]
        inv = 1.0 / (es[0] + es[1] + es[2])
        ws = [e * inv for e in es]
        parts = []
        for h in range(HEADS):
            sl = slice(h * HEAD_DIM, (h + 1) * HEAD_DIM)
            acc = ws[0][:, h:h + 1] * o0[0, 0, :, sl].astype(F32)
            acc = acc + ws[1][:, h:h + 1] * on1[h]
            acc = acc + ws[2][:, h:h + 1] * on2[h]
            parts.append(acc.astype(BF16))
        y_in = jnp.concatenate(parts, axis=1)
    else:
        (y_ref, w_ref, x_ref, gate_ref, g_ref, b_ref, sc_ref, sh_ref, wr_ref, br_ref,
         xn_ref, hs_ref, lg_ref) = refs
        y_in = y_ref[...]
    y = jnp.dot(y_in, w_ref[...], preferred_element_type=F32)
    xn = _deepnorm(x_ref[...], y, gate_ref[0], g_ref[...], b_ref[...])
    xn_ref[...] = xn
    h = xn * (1.0 + sc_ref[0]) + sh_ref[0]
    _store_slabs(hs_ref, h)
    lg_ref[...] = _router_logits(h, wr_ref[...], br_ref[...])


def _proj_ln_router(mix_in, w_bf, x2, gate, ln_g, ln_b, scale, shift, wr, br, *, merge):
    tm = PROJ_TM
    per_b = SEQ // tm
    row = lambda i: (i, 0)
    full = lambda i: (0, 0)
    per_batch = lambda i: (i // per_b, 0, 0)
    scratch = []
    if merge:
        def resid(width, dil):
            return pl.BlockSpec((1, dil, tm // dil, width), lambda i: (i // per_b, 0, i % per_b, 0))
        mix_specs = ([resid(D_MODEL, dil) for dil in DILATIONS] + [resid(LANES, dil) for dil in DILATIONS])
        scratch = [pltpu.VMEM((HEADS, tm, HEAD_DIM), F32), pltpu.VMEM((HEADS, tm, HEAD_DIM), F32),
                   pltpu.VMEM((tm, LANES), F32), pltpu.VMEM((tm, LANES), F32)]
    else:
        mix_specs = [pl.BlockSpec((tm, D_MODEL), row)]
    return pl.pallas_call(
        functools.partial(_proj_kernel, merge=merge),
        grid=(TOKENS // tm,),
        in_specs=mix_specs + [
            pl.BlockSpec((D_MODEL, D_MODEL), full),
            pl.BlockSpec((tm, D_MODEL), row),
            pl.BlockSpec((1, 1, D_MODEL), per_batch),
            pl.BlockSpec((1, D_MODEL), full),
            pl.BlockSpec((1, D_MODEL), full),
            pl.BlockSpec((1, 1, D_MODEL), per_batch),
            pl.BlockSpec((1, 1, D_MODEL), per_batch),
            pl.BlockSpec((D_MODEL, LANES), full),
            pl.BlockSpec((1, LANES), full)],
        out_specs=[pl.BlockSpec((tm, D_MODEL), row),
                   pl.BlockSpec((tm * SLABS, LANES), row),
                   pl.BlockSpec((tm, LANES), row)],
        out_shape=[jax.ShapeDtypeStruct((TOKENS, D_MODEL), F32),
                   jax.ShapeDtypeStruct((TOKENS * SLABS, LANES), U32),
                   jax.ShapeDtypeStruct((TOKENS, LANES), F32)],
        scratch_shapes=scratch,
        compiler_params=_cparams(("arbitrary",)),
        name="proj_merge" if merge else "proj",
    )(*mix_in, w_bf, x2, gate, ln_g, ln_b, scale, shift, wr, br)


def _slab_rows(row):
    if isinstance(row, int):
        return pl.ds(row * SLABS, SLABS)
    return pl.ds(pl.multiple_of(row * SLABS, SLABS), SLABS)


def _row_copy(src, dst, s_row, d_row, sem):
    return pltpu.make_async_copy(src.at[_slab_rows(s_row)], dst.at[_slab_rows(d_row)], sem)


def _moe_kernel(be_ref, nv_ref, rt_ref, hs_hbm, wg_ref, wu_ref, wd_ref, ys_ref, xbuf, wg_bf, wu_bf, wd_bf, sems):
    i = pl.program_id(0)
    n_blk = pl.num_programs(0)
    nv = nv_ref[i]

    def gather(blk, slot):
        def issue(k, carry):
            _row_copy(hs_hbm, xbuf, rt_ref[blk * MOE_BLK + k], slot * MOE_BLK + k, sems.at[slot]).start()
            return carry
        lax.fori_loop(0, MOE_BLK, issue, 0, unroll=8)

    @pl.when(i == 0)
    def _():
        gather(0, 0)

    nxt = jnp.minimum(i + 1, n_blk - 1)

    @pl.when((i + 1 < n_blk) & (nv_ref[nxt] > 0))
    def _():
        gather(nxt, nxt % 2)

    @pl.when(nv > 0)
    def _():
        prev = be_ref[jnp.maximum(i - 1, 0)]

        @pl.when((i == 0) | (be_ref[i] != prev))
        def _():
            wg_bf[...] = wg_ref[0, 0].astype(BF16)
            wu_bf[...] = wu_ref[0, 0].astype(BF16)
            wd_bf[...] = wd_ref[0, 0].astype(BF16)

        slot = i % 2

        def drain(k, carry):
            _row_copy(hs_hbm, xbuf, 0, 0, sems.at[slot]).wait()
            return carry
        lax.fori_loop(0, MOE_BLK, drain, 0, unroll=8)

        x = _load_slabs(xbuf, pl.multiple_of(slot * (MOE_BLK * SLABS), MOE_BLK * SLABS), MOE_BLK)
        rows = lax.broadcasted_iota(jnp.int32, (MOE_BLK, 1), 0)
        x = jnp.where(rows < nv, x, 0.0).astype(BF16)
        g = jnp.dot(x, wg_bf[...], preferred_element_type=F32)
        u = jnp.dot(x, wu_bf[...], preferred_element_type=F32)
        a = (g * (1.0 / (1.0 + jnp.exp(-g)))) * u
        y = jnp.dot(a.astype(BF16), wd_bf[...], preferred_element_type=F32)
        _store_slabs(ys_ref, y)

    @pl.when(nv <= 0)
    def _():
        ys_ref[...] = jnp.zeros_like(ys_ref)


def _moe_ffn(block_e, block_nv, row_tok, hs, w_gate, w_up, w_down, layer):
    wmap = lambda i, be, nv, rt: (layer, be[i], 0, 0)
    return pl.pallas_call(
        _moe_kernel,
        grid_spec=pltpu.PrefetchScalarGridSpec(
            num_scalar_prefetch=3,
            grid=(MOE_NB,),
            in_specs=[pl.BlockSpec(memory_space=pl.ANY),
                      pl.BlockSpec((1, 1, D_MODEL, EXPERT_DIM), wmap),
                      pl.BlockSpec((1, 1, D_MODEL, EXPERT_DIM), wmap),
                      pl.BlockSpec((1, 1, EXPERT_DIM, D_MODEL), wmap)],
            out_specs=pl.BlockSpec((MOE_BLK * SLABS, LANES), lambda i, be, nv, rt: (i, 0)),
            scratch_shapes=[pltpu.VMEM((2 * MOE_BLK * SLABS, LANES), U32),
                            pltpu.VMEM((D_MODEL, EXPERT_DIM), BF16),
                            pltpu.VMEM((D_MODEL, EXPERT_DIM), BF16),
                            pltpu.VMEM((EXPERT_DIM, D_MODEL), BF16),
                            pltpu.SemaphoreType.DMA((2,))]),
        out_shape=jax.ShapeDtypeStruct((MOE_ROWS * SLABS, LANES), U32),
        compiler_params=pltpu.CompilerParams(dimension_semantics=("arbitrary",), vmem_limit_bytes=VMEM_LIMIT,
                                             disable_bounds_checks=True),
        name="moe_ffn",
    )(block_e, block_nv, row_tok, hs, w_gate, w_up, w_down)


COMB_TM = 256


def _combine_kernel(dest_ref, ys_hbm, gt_ref, x_ref, gate_ref, g_ref, b_ref, *rest, modulate):
    if modulate:
        sc_ref, sh_ref, xn_ref, hn_ref, buf, sem = rest
    else:
        xn_ref, buf, sem = rest
    i = pl.program_id(0)
    tm = COMB_TM

    def copy(src_row, slot):
        return _row_copy(ys_hbm, buf, src_row, slot, sem)

    def issue(t, carry):
        tok = i * tm + t
        copy(dest_ref[2 * tok], t).start()
        copy(dest_ref[2 * tok + 1], tm + t).start()
        return carry

    lax.fori_loop(0, tm, issue, 0)

    def drain(t, carry):
        copy(0, 0).wait()
        copy(0, 0).wait()
        return carry

    lax.fori_loop(0, tm, drain, 0)

    gt = gt_ref[...]
    y = _load_slabs(buf, 0, tm) * gt[:, 0:1] + _load_slabs(buf, tm * SLABS, tm) * gt[:, 1:2]
    xn = _deepnorm(x_ref[...], y, gate_ref[0], g_ref[...], b_ref[...])
    xn_ref[...] = xn
    if modulate:
        hn_ref[...] = (xn * (1.0 + sc_ref[0]) + sh_ref[0]).astype(BF16)


def _combine(dest, ys, gates, x2, gate, ln_g, ln_b, scale=None, shift=None):
    tm = COMB_TM
    per_b = SEQ // tm
    modulate = scale is not None
    row = lambda i, d: (i, 0)
    full = lambda i, d: (0, 0)
    per_batch = lambda i, d: (i // per_b, 0, 0)
    in_specs = [pl.BlockSpec(memory_space=pl.ANY),
                pl.BlockSpec((tm, 2), row),
                pl.BlockSpec((tm, D_MODEL), row),
                pl.BlockSpec((1, 1, D_MODEL), per_batch),
                pl.BlockSpec((1, D_MODEL), full),
                pl.BlockSpec((1, D_MODEL), full)]
    out_specs = [pl.BlockSpec((tm, D_MODEL), row)]
    out_shape = [jax.ShapeDtypeStruct((TOKENS, D_MODEL), F32)]
    args = [dest, ys, gates, x2, gate, ln_g, ln_b]
    if modulate:
        in_specs += [pl.BlockSpec((1, 1, D_MODEL), per_batch)] * 2
        out_specs.append(pl.BlockSpec((tm, D_MODEL), row))
        out_shape.append(jax.ShapeDtypeStruct((TOKENS, D_MODEL), BF16))
        args += [scale, shift]
    return pl.pallas_call(
        functools.partial(_combine_kernel, modulate=modulate),
        grid_spec=pltpu.PrefetchScalarGridSpec(
            num_scalar_prefetch=1,
            grid=(TOKENS // tm,),
            in_specs=in_specs,
            out_specs=out_specs,
            scratch_shapes=[pltpu.VMEM((2 * tm * SLABS, LANES), U32), pltpu.SemaphoreType.DMA(())]),
        out_shape=out_shape,
        compiler_params=_cparams(("arbitrary",)),
        name="moe_combine",
    )(*args)


def _route(logits):
    lc = logits[:, :N_GROUPS]
    lf = logits[:, N_GROUPS:N_GROUPS + N_EXPERTS].reshape(TOKENS, N_GROUPS, EPG)
    pc = jax.nn.softmax(lc, axis=-1)
    pg = jnp.max(pc, axis=-1, keepdims=True)
    gi = jnp.argmax(pc, axis=-1)[:, None]
    lf_sel = jnp.take_along_axis(lf, gi[:, :, None], axis=1)[:, 0]
    pf = jax.nn.softmax(lf_sel, axis=-1)
    e1 = jnp.argmax(pf, axis=-1)[:, None]
    p1 = jnp.max(pf, axis=-1, keepdims=True)
    pf2 = jnp.where(jnp.arange(EPG)[None, :] == e1, -1.0, pf)
    e2 = jnp.argmax(pf2, axis=-1)[:, None]
    p2 = jnp.max(pf2, axis=-1, keepdims=True)
    pk = jnp.concatenate([p1, p2], axis=1)
    ek = jnp.concatenate([e1, e2], axis=1)
    gates = pg * (pk / jnp.sum(pk, axis=-1, keepdims=True))
    e_flat = (gi * EPG + ek).reshape(-1).astype(jnp.int32)
    onehot = (e_flat[:, None] == jnp.arange(N_EXPERTS, dtype=jnp.int32)[None, :]).astype(jnp.int32)
    csum = jnp.cumsum(onehot, axis=0)
    rank = jnp.sum((csum - onehot) * onehot, axis=1)
    counts = csum[-1]
    padded = (counts + MOE_BLK - 1) // MOE_BLK * MOE_BLK
    pends = jnp.cumsum(padded)
    pstarts = pends - padded
    dest = (pstarts[e_flat] + rank).astype(jnp.int32)
    blk_start = jnp.arange(MOE_NB, dtype=jnp.int32) * MOE_BLK
    block_e = jnp.clip(jnp.searchsorted(pends, blk_start, side='right'), 0, N_EXPERTS - 1).astype(jnp.int32)
    block_nv = jnp.clip(counts[block_e] - (blk_start - pstarts[block_e]), 0, MOE_BLK)
    block_nv = jnp.where(blk_start < pends[-1], block_nv, 0).astype(jnp.int32)
    row_tok = jnp.zeros((MOE_ROWS,), jnp.int32).at[dest].set(jnp.arange(2 * TOKENS, dtype=jnp.int32) // 2)
    return gates, dest, row_tok, block_e, block_nv


def _moe(hs, logits, x2, gate, ln_g, ln_b, w_gate, w_up, w_down, layer, scale=None, shift=None):
    gates, dest, row_tok, block_e, block_nv = _route(logits)
    ys = _moe_ffn(block_e, block_nv, row_tok, hs, w_gate, w_up, w_down, layer)
    return _combine(dest, ys, gates, x2, gate, ln_g, ln_b, scale, shift)


FN_TM = 256


def _fnet_in_kernel(h_ref, w_ref, cs_ref, ab_ref):
    u = jnp.dot(h_ref[...], w_ref[...], preferred_element_type=F32).astype(BF16)
    for g in range(FNET_GROUPS):
        sl = slice(g * FNET_CH, (g + 1) * FNET_CH)
        ab = jnp.dot(u[:, sl], cs_ref[...], preferred_element_type=F32)
        ab_ref[0, 0, :, sl] = ab[:, :FNET_CH].astype(BF16)
        ab_ref[0, 1, :, sl] = ab[:, FNET_CH:].astype(BF16)


def _fnet_in(h_bf, w_in_bf, cs):
    tm = FN_TM
    per_b = SEQ // tm
    return pl.pallas_call(
        _fnet_in_kernel,
        grid=(TOKENS // tm,),
        in_specs=[pl.BlockSpec((tm, D_MODEL), lambda i: (i, 0)),
                  pl.BlockSpec((D_MODEL, D_MODEL), lambda i: (0, 0)),
                  pl.BlockSpec((FNET_CH, 2 * FNET_CH), lambda i: (0, 0))],
        out_specs=pl.BlockSpec((1, 2, tm, D_MODEL), lambda i: (i // per_b, 0, i % per_b, 0)),
        out_shape=jax.ShapeDtypeStruct((BATCH, 2, SEQ, D_MODEL), BF16),
        compiler_params=_cparams(("arbitrary",)),
        name="fnet_in",
    )(h_bf, w_in_bf, cs)


DFT_R = 4
DFT_Q = SEQ // DFT_R
DFT_TN = 256
DFT_CHUNK = 256
_C4 = (1, 0, -1, 0)
_S4 = (0, 1, 0, -1)


def _signed_sum(terms):
    acc = None
    for sign, v in terms:
        if sign == 0:
            continue
        if acc is None:
            acc = v if sign > 0 else -v
        else:
            acc = acc + v if sign > 0 else acc - v
    return acc


def _dft_kernel(a_ref, b_ref, m_hbm, y_ref, m_scr, ab_scr, y_scr, sem):
    @pl.when((pl.program_id(0) == 0) & (pl.program_id(1) == 0))
    def _():
        cp = pltpu.make_async_copy(m_hbm, m_scr, sem)
        cp.start()
        cp.wait()

    for ch in range(DFT_Q // DFT_CHUNK):
        rows = [slice(m * DFT_Q + ch * DFT_CHUNK, m * DFT_Q + (ch + 1) * DFT_CHUNK) for m in range(DFT_R)]
        a = [a_ref[0, 0, r, :].astype(F32) for r in rows]
        b = [b_ref[0, 0, r, :].astype(F32) for r in rows]
        for rho in range(DFT_R):
            q = [(rho * m) % 4 for m in range(DFT_R)]
            ap = _signed_sum([(_C4[q[m]], a[m]) for m in range(DFT_R)] + [(-_S4[q[m]], b[m]) for m in range(DFT_R)])
            bp = _signed_sum([(_S4[q[m]], a[m]) for m in range(DFT_R)] + [(_C4[q[m]], b[m]) for m in range(DFT_R)])
            ab_scr[rho, ch * DFT_CHUNK:(ch + 1) * DFT_CHUNK, :] = ap.astype(BF16)
            ab_scr[rho, DFT_Q + ch * DFT_CHUNK:DFT_Q + (ch + 1) * DFT_CHUNK, :] = bp.astype(BF16)

    for rho in range(DFT_R):
        y = jnp.dot(m_scr[rho], ab_scr[rho], preferred_element_type=F32)
        for c in range(DFT_TN // LANES):
            y_scr[c, pl.ds(rho, DFT_Q, stride=DFT_R), :] = y[:, c * LANES:(c + 1) * LANES]
    for c in range(DFT_TN // LANES):
        y_ref[0, :, c * LANES:(c + 1) * LANES] = y_scr[c].astype(BF16)


def _seq_dft(mats, ab):
    tn = DFT_TN
    return pl.pallas_call(
        _dft_kernel,
        grid=(BATCH, D_MODEL // tn),
        in_specs=[pl.BlockSpec((1, 1, SEQ, tn), lambda b, j: (b, 0, 0, j)),
                  pl.BlockSpec((1, 1, SEQ, tn), lambda b, j: (b, 1, 0, j)),
                  pl.BlockSpec(memory_space=pl.ANY)],
        out_specs=pl.BlockSpec((1, SEQ, tn), lambda b, j: (b, 0, j)),
        out_shape=jax.ShapeDtypeStruct((BATCH, SEQ, D_MODEL), BF16),
        scratch_shapes=[pltpu.VMEM((DFT_R, DFT_Q, 2 * DFT_Q), BF16),
                        pltpu.VMEM((DFT_R, 2 * DFT_Q, tn), BF16),
                        pltpu.VMEM((tn // LANES, SEQ, LANES), F32),
                        pltpu.SemaphoreType.DMA(())],
        compiler_params=_cparams(("arbitrary", "arbitrary")),
        name="seq_dft",
    )(ab, ab, mats)


def _dft_constants():
    n = np.arange(FNET_CH)
    ang = 2.0 * np.pi * ((n[:, None] * n[None, :]) % FNET_CH) / FNET_CH
    cs = np.concatenate([np.cos(ang), np.sin(ang)], axis=1) / math.sqrt(FNET_CH)
    return jnp.asarray(cs, dtype=BF16)


def _seq_dft_matrices():
    shape = (DFT_R, DFT_Q, DFT_Q)
    rho = lax.broadcasted_iota(jnp.int32, shape, 0)
    kp = lax.broadcasted_iota(jnp.int32, shape, 1)
    jp = lax.broadcasted_iota(jnp.int32, shape, 2)
    ang = (((DFT_R * kp + rho) * jp) & (SEQ - 1)).astype(F32) * (2.0 * math.pi / SEQ)
    norm = 1.0 / math.sqrt(SEQ)
    return jnp.concatenate([(jnp.cos(ang) * norm).astype(BF16), (jnp.sin(ang) * (-norm)).astype(BF16)], axis=2)


def kernel(x, c, positions, ada_w, ada_b, attn_w_qkv, attn_w_o, fnet_w_in, fnet_w_out, ln_g, ln_b,
           router_coarse_w, router_coarse_b, router_fine_w, router_fine_b,
           expert_w_gate, expert_w_up, expert_w_down):
    x2 = x.reshape(TOKENS, D_MODEL)

    c8 = jnp.zeros((8, D_MODEL), F32).at[:BATCH].set(c)
    mod = _adaln(c8, ada_w.reshape(4, D_MODEL, 3 * D_MODEL), ada_b.reshape(4, 1, 3 * D_MODEL))[:, :BATCH]

    def mod_rows(idx):
        m = mod[idx].reshape(BATCH, 3, 1, D_MODEL)
        return m[:, 0], m[:, 1], m[:, 2]

    def router_params(i):
        wr = jnp.zeros((D_MODEL, LANES), F32)
        wr = wr.at[:, :N_GROUPS].set(router_coarse_w[i]).at[:, N_GROUPS:N_GROUPS + N_EXPERTS].set(router_fine_w[i])
        br = jnp.zeros((1, LANES), F32)
        br = br.at[0, :N_GROUPS].set(router_coarse_b[i]).at[0, N_GROUPS:N_GROUPS + N_EXPERTS].set(router_fine_b[i])
        return wr, br

    inv_freq = np.float32(ROPE_THETA) ** (-np.arange(0, ROT_DIM, 2, dtype=np.float32) / np.float32(ROT_DIM))
    invf = np.zeros((1, LANES), np.float32)
    invf[0, :ROT_HALF] = inv_freq
    invf[0, ROT_HALF:ROT_DIM] = inv_freq
    tabs = _rot_tables(positions.reshape(TOKENS, 1), jnp.asarray(invf))

    shift, scale, gate = mod_rows(0)
    outs, lses = [], []
    for g, dil in enumerate(DILATIONS):
        o_g, lse_g = _attention_group(_qkv(x2, scale, shift, attn_w_qkv, tabs, g, dil), dil)
        outs.append(o_g)
        lses.append(lse_g)
    shift2, scale2, gate2 = mod_rows(1)
    wr, br = router_params(0)
    x2, hs, logits = _proj_ln_router(outs + lses, attn_w_o[0].astype(BF16), x2, gate, ln_g[0, 0][None], ln_b[0, 0][None],
                                     scale2, shift2, wr, br, merge=True)
    shift3, scale3, gate3 = mod_rows(2)
    x2, h_bf = _moe(hs, logits, x2, gate2, ln_g[0, 1][None], ln_b[0, 1][None],
                    expert_w_gate, expert_w_up, expert_w_down, 0, scale3, shift3)

    ab = _fnet_in(h_bf, fnet_w_in[0].astype(BF16), _dft_constants())
    y = _seq_dft(_seq_dft_matrices(), ab).reshape(TOKENS, D_MODEL)
    shift4, scale4, gate4 = mod_rows(3)
    wr, br = router_params(1)
    x2, hs, logits = _proj_ln_router([y], fnet_w_out[0].astype(BF16), x2, gate3, ln_g[1, 0][None], ln_b[1, 0][None],
                                     scale4, shift4, wr, br, merge=False)
    (x2,) = _moe(hs, logits, x2, gate4, ln_g[1, 1][None], ln_b[1, 1][None],
                 expert_w_gate, expert_w_up, expert_w_down, 1)
    return x2.reshape(BATCH, SEQ, D_MODEL)
```

```python
import functools
import math

import numpy as np
import jax
import jax.numpy as jnp
from jax import lax
from jax.experimental import pallas as pl
from jax.experimental.pallas import tpu as pltpu

F32 = jnp.float32
BF16 = jnp.bfloat16
U32 = jnp.uint32

D_MODEL = 2048
BATCH = 2
SEQ = 4096
TOKENS = BATCH * SEQ
DEPTH = 2
HEAD_DIM = 128
HEADS = D_MODEL // HEAD_DIM
DILATIONS = (1, 4, 16)
RADIUS = 64
N_DIL = 3
QKV_COLS = N_DIL * 3 * D_MODEL
ROT_DIM = HEAD_DIM // 4
ROT_HALF = ROT_DIM // 2
ROPE_THETA = 500000.0
FNET_GROUPS = 4
FNET_CH = D_MODEL // FNET_GROUPS
N_GROUPS = 4
EPG = 8
N_EXPERTS = N_GROUPS * EPG
EXPERT_DIM = D_MODEL // 4
LN_EPS = 1e-5
NEG_INF = -1e30
ALPHA = (2 * DEPTH) ** 0.25

LANES = 128
SLABS = D_MODEL // (2 * LANES)
VMEM_LIMIT = 56 * 1024 * 1024

MOE_BLK = 256
MOE_NB = (2 * TOKENS) // MOE_BLK + N_EXPERTS
MOE_ROWS = MOE_NB * MOE_BLK


def _cparams(sem):
    return pltpu.CompilerParams(dimension_semantics=sem, vmem_limit_bytes=VMEM_LIMIT)


def _adaln_kernel(c_ref, w_ref, b_ref, o_ref):
    c = c_ref[...]
    sc = (c * (1.0 / (1.0 + jnp.exp(-c)))).astype(BF16)
    m = jnp.dot(sc, w_ref[0].astype(BF16), preferred_element_type=F32)
    o_ref[0] = m + b_ref[0]


def _adaln(c8, w4, b4):
    tn = 768
    n = 3 * D_MODEL
    return pl.pallas_call(
        _adaln_kernel,
        grid=(4, n // tn),
        in_specs=[pl.BlockSpec((8, D_MODEL), lambda s, j: (0, 0)),
                  pl.BlockSpec((1, D_MODEL, tn), lambda s, j: (s, 0, j)),
                  pl.BlockSpec((1, 1, tn), lambda s, j: (s, 0, j))],
        out_specs=pl.BlockSpec((1, 8, tn), lambda s, j: (s, 0, j)),
        out_shape=jax.ShapeDtypeStruct((4, 8, n), F32),
        compiler_params=_cparams(("arbitrary", "arbitrary")),
        name="adaln",
    )(c8, w4, b4)


def _rot_tab_kernel(pos_ref, invf_ref, o_ref):
    ang = pos_ref[...].astype(F32) * invf_ref[...]
    lane = lax.broadcasted_iota(jnp.int32, ang.shape, 1)
    c = jnp.where(lane < ROT_DIM, jnp.cos(ang), 1.0)
    s = jnp.sin(ang)
    s_lo = jnp.where(lane < ROT_HALF, -s, 0.0)
    s_hi = jnp.where((lane >= ROT_HALF) & (lane < ROT_DIM), s, 0.0)
    qs = HEAD_DIM ** -0.5
    o_ref[0] = c * qs
    o_ref[1] = s_lo * qs
    o_ref[2] = s_hi * qs
    o_ref[3] = c
    o_ref[4] = s_lo
    o_ref[5] = s_hi
    o_ref[6] = jnp.ones_like(c)
    o_ref[7] = jnp.zeros_like(c)
    o_ref[8] = jnp.zeros_like(c)


def _rot_tables(pos_col, invf):
    tm = 1024
    return pl.pallas_call(
        _rot_tab_kernel,
        grid=(TOKENS // tm,),
        in_specs=[pl.BlockSpec((tm, 1), lambda i: (i, 0)),
                  pl.BlockSpec((1, LANES), lambda i: (0, 0))],
        out_specs=pl.BlockSpec((9, tm, LANES), lambda i: (0, i, 0)),
        out_shape=jax.ShapeDtypeStruct((9, TOKENS, LANES), F32),
        compiler_params=_cparams(("arbitrary",)),
        name="rot_tables",
    )(pos_col, invf)


QKV_TM = 1024
QKV_TN = 512


def _rotate(t, c, s_lo, s_hi):
    return (t * c + pltpu.roll(t, LANES - ROT_HALF, axis=1) * s_lo
            + pltpu.roll(t, ROT_HALF, axis=1) * s_hi)


def _qkv_kernel(x_ref, sc_ref, sh_ref, w_ref, tab_ref, o_ref, h_ref, r_ref, *, dil):
    j = pl.program_id(1)

    @pl.when(j == 0)
    def _():
        h_ref[...] = (x_ref[...] * (1.0 + sc_ref[0]) + sh_ref[0]).astype(BF16)

    acc = jnp.dot(h_ref[...], w_ref[0].astype(BF16), preferred_element_type=F32)
    rows = QKV_TM // dil
    for hc in range(QKV_TN // LANES):
        sl = slice(hc * LANES, (hc + 1) * LANES)
        rot = _rotate(acc[:, sl], tab_ref[0], tab_ref[1], tab_ref[2])
        if dil == 1:
            o_ref[0, 0, :, sl] = rot.astype(BF16)
        else:
            r_ref[hc] = rot
            for r in range(dil):
                o_ref[0, r, :, sl] = r_ref[hc, pl.ds(r, rows, stride=dil), :].astype(BF16)


def _qkv(x2, scale, shift, w_qkv, tabs, g, dil):
    tm, tn = QKV_TM, QKV_TN
    per_b = SEQ // tm
    n_j = 3 * D_MODEL // tn
    return pl.pallas_call(
        functools.partial(_qkv_kernel, dil=dil),
        grid=(TOKENS // tm, n_j),
        in_specs=[pl.BlockSpec((tm, D_MODEL), lambda i, j: (i, 0)),
                  pl.BlockSpec((1, 1, D_MODEL), lambda i, j: (i // per_b, 0, 0)),
                  pl.BlockSpec((1, 1, D_MODEL), lambda i, j: (i // per_b, 0, 0)),
                  pl.BlockSpec((1, D_MODEL, tn), lambda i, j: (0, 0, g * n_j + j)),
                  pl.BlockSpec((3, tm, LANES), lambda i, j: (j // (D_MODEL // tn), i, 0))],
        out_specs=pl.BlockSpec((1, dil, tm // dil, tn), lambda i, j: (i // per_b, 0, i % per_b, j)),
        out_shape=jax.ShapeDtypeStruct((BATCH, dil, SEQ // dil, 3 * D_MODEL), BF16),
        scratch_shapes=[pltpu.VMEM((tm, D_MODEL), BF16), pltpu.VMEM((tn // LANES, tm, LANES), F32)],
        compiler_params=_cparams(("arbitrary", "arbitrary")),
        name=f"qkv_proj_dil{dil}",
    )(x2, scale, shift, w_qkv, tabs)


ATT_TQ = 128
ATT_TK = ATT_TQ + 2 * RADIUS


def _attn_kernel(q_ref, kp_ref, kc_ref, kn_ref, vp_ref, vc_ref, vn_ref, o_ref, lse_ref, kbuf, vbuf, *, seq_len):
    i = pl.program_id(2)
    kbuf[0:RADIUS] = kp_ref[0, 0]
    kbuf[RADIUS:RADIUS + ATT_TQ] = kc_ref[0, 0]
    kbuf[RADIUS + ATT_TQ:ATT_TK] = kn_ref[0, 0]
    vbuf[0:RADIUS] = vp_ref[0, 0]
    vbuf[RADIUS:RADIUS + ATT_TQ] = vc_ref[0, 0]
    vbuf[RADIUS + ATT_TQ:ATT_TK] = vn_ref[0, 0]

    qpos = i * ATT_TQ + lax.broadcasted_iota(jnp.int32, (ATT_TQ, ATT_TK), 0)
    kpos = i * ATT_TQ - RADIUS + lax.broadcasted_iota(jnp.int32, (ATT_TQ, ATT_TK), 1)
    valid = (jnp.abs(qpos - kpos) <= RADIUS) & (kpos >= 0) & (kpos < seq_len)
    bias = jnp.where(valid, 0.0, NEG_INF).astype(F32)
    lane = lax.broadcasted_iota(jnp.int32, (ATT_TQ, LANES), 1)
    lse_all = jnp.zeros((ATT_TQ, LANES), F32)
    for h in range(HEADS):
        sl = slice(h * HEAD_DIM, (h + 1) * HEAD_DIM)
        s = lax.dot_general(q_ref[0, 0, :, sl], kbuf[:, sl], (((1,), (1,)), ((), ())),
                            preferred_element_type=F32) + bias
        m = jnp.max(s, axis=-1, keepdims=True)
        p = jnp.exp(s - m)
        den = jnp.sum(p, axis=-1, keepdims=True)
        o = jnp.dot(p.astype(BF16), vbuf[:, sl], preferred_element_type=F32) / den
        o_ref[0, 0, :, sl] = o.astype(BF16)
        lse_all = jnp.where(lane == h, m + jnp.log(den), lse_all)
    lse_ref[0, 0] = lse_all


def _attention_group(qkv, dil):
    seq_len = SEQ // dil
    tq = ATT_TQ
    sub = tq // RADIUS
    n_sub = seq_len // RADIUS

    def cur(which):
        return pl.BlockSpec((1, 1, tq, D_MODEL), lambda b, r, i: (b, r, i, which))

    def prev(which):
        return pl.BlockSpec((1, 1, RADIUS, D_MODEL), lambda b, r, i: (b, r, jnp.maximum(i * sub - 1, 0), which))

    def nxt(which):
        return pl.BlockSpec((1, 1, RADIUS, D_MODEL),
                            lambda b, r, i: (b, r, jnp.minimum((i + 1) * sub, n_sub - 1), which))

    return pl.pallas_call(
        functools.partial(_attn_kernel, seq_len=seq_len),
        grid=(BATCH, dil, seq_len // tq),
        in_specs=[cur(0), prev(1), cur(1), nxt(1), prev(2), cur(2), nxt(2)],
        out_specs=[pl.BlockSpec((1, 1, tq, D_MODEL), lambda b, r, i: (b, r, i, 0)),
                   pl.BlockSpec((1, 1, tq, LANES), lambda b, r, i: (b, r, i, 0))],
        out_shape=[jax.ShapeDtypeStruct((BATCH, dil, seq_len, D_MODEL), BF16),
                   jax.ShapeDtypeStruct((BATCH, dil, seq_len, LANES), F32)],
        scratch_shapes=[pltpu.VMEM((ATT_TK, D_MODEL), BF16), pltpu.VMEM((ATT_TK, D_MODEL), BF16)],
        compiler_params=_cparams(("arbitrary", "arbitrary", "arbitrary")),
        name=f"attn_dil{dil}",
    )(qkv, qkv, qkv, qkv, qkv, qkv, qkv)


def _deepnorm(x, y, gate, g, b):
    z = ALPHA * x + gate * y
    mu = jnp.mean(z, axis=-1, keepdims=True)
    zc = z - mu
    var = jnp.mean(zc * zc, axis=-1, keepdims=True)
    return zc * lax.rsqrt(var + LN_EPS) * g + b


def _split_bf16(a):
    hi = a.astype(BF16)
    lo = (a - hi.astype(F32)).astype(BF16)
    return hi, lo


def _router_logits(h, wr, br):
    h_hi, h_lo = _split_bf16(h)
    w_hi, w_lo = _split_bf16(wr)
    acc = jnp.dot(h_hi, w_hi, preferred_element_type=F32)
    acc = acc + jnp.dot(h_hi, w_lo, preferred_element_type=F32)
    acc = acc + jnp.dot(h_lo, w_hi, preferred_element_type=F32)
    return acc + br


def _store_slabs(ref, val):
    tm = val.shape[0]
    for s in range(SLABS):
        lo = val[:, (2 * s) * LANES:(2 * s + 1) * LANES].astype(BF16).astype(F32)
        hi = val[:, (2 * s + 1) * LANES:(2 * s + 2) * LANES].astype(BF16).astype(F32)
        word = ((lax.bitcast_convert_type(lo, U32) >> 16)
                | (lax.bitcast_convert_type(hi, U32) & jnp.uint32(0xFFFF0000)))
        ref[pl.ds(s, tm, stride=SLABS), :] = word


def _load_slabs(ref, base, tm):
    parts = []
    for s in range(SLABS):
        word = ref[pl.ds(base + s, tm, stride=SLABS), :]
        parts.append(lax.bitcast_convert_type(word << 16, F32))
        parts.append(lax.bitcast_convert_type(word & jnp.uint32(0xFFFF0000), F32))
    return jnp.concatenate(parts, axis=1)


PROJ_TM = 256


def _proj_kernel(*refs, merge):
    if merge:
        (o0, o1, o2, l0, l1, l2, w_ref, x_ref, gate_ref, g_ref, b_ref, sc_ref, sh_ref, wr_ref, br_ref,
         xn_ref, hs_ref, lg_ref, on1, on2, ln1, ln2) = refs
        for src, dst, dil in ((o1, on1, DILATIONS[1]), (o2, on2, DILATIONS[2])):
            for r in range(dil):
                for h in range(HEADS):
                    dst[h, pl.ds(r, PROJ_TM // dil, stride=dil), :] = (
                        src[0, r, :, h * HEAD_DIM:(h + 1) * HEAD_DIM].astype(F32))
        for src, dst, dil in ((l1, ln1, DILATIONS[1]), (l2, ln2, DILATIONS[2])):
            for r in range(dil):
                dst[pl.ds(r, PROJ_TM // dil, stride=dil), :] = src[0, r]
        ls = [l0[0, 0], ln1[...], ln2[...]]
        mx = jnp.maximum(jnp.maximum(ls[0], ls[1]), ls[2])
        es = [jnp.exp(l - mx) for l in ls]
        inv = 1.0 / (es[0] + es[1] + es[2])
        ws = [e * inv for e in es]
        parts = []
        for h in range(HEADS):
            sl = slice(h * HEAD_DIM, (h + 1) * HEAD_DIM)
            acc = ws[0][:, h:h + 1] * o0[0, 0, :, sl].astype(F32)
            acc = acc + ws[1][:, h:h + 1] * on1[h]
            acc = acc + ws[2][:, h:h + 1] * on2[h]
            parts.append(acc.astype(BF16))
        y_in = jnp.concatenate(parts, axis=1)
    else:
        (y_ref, w_ref, x_ref, gate_ref, g_ref, b_ref, sc_ref, sh_ref, wr_ref, br_ref,
         xn_ref, hs_ref, lg_ref) = refs
        y_in = y_ref[...]
    y = jnp.dot(y_in, w_ref[...], preferred_element_type=F32)
    xn = _deepnorm(x_ref[...], y, gate_ref[0], g_ref[...], b_ref[...])
    xn_ref[...] = xn
    h = xn * (1.0 + sc_ref[0]) + sh_ref[0]
    _store_slabs(hs_ref, h)
    lg_ref[...] = _router_logits(h, wr_ref[...], br_ref[...])


def _proj_ln_router(mix_in, w_bf, x2, gate, ln_g, ln_b, scale, shift, wr, br, *, merge):
    tm = PROJ_TM
    per_b = SEQ // tm
    row = lambda i: (i, 0)
    full = lambda i: (0, 0)
    per_batch = lambda i: (i // per_b, 0, 0)
    scratch = []
    if merge:
        def resid(width, dil):
            return pl.BlockSpec((1, dil, tm // dil, width), lambda i: (i // per_b, 0, i % per_b, 0))
        mix_specs = ([resid(D_MODEL, dil) for dil in DILATIONS] + [resid(LANES, dil) for dil in DILATIONS])
        scratch = [pltpu.VMEM((HEADS, tm, HEAD_DIM), F32), pltpu.VMEM((HEADS, tm, HEAD_DIM), F32),
                   pltpu.VMEM((tm, LANES), F32), pltpu.VMEM((tm, LANES), F32)]
    else:
        mix_specs = [pl.BlockSpec((tm, D_MODEL), row)]
    return pl.pallas_call(
        functools.partial(_proj_kernel, merge=merge),
        grid=(TOKENS // tm,),
        in_specs=mix_specs + [
            pl.BlockSpec((D_MODEL, D_MODEL), full),
            pl.BlockSpec((tm, D_MODEL), row),
            pl.BlockSpec((1, 1, D_MODEL), per_batch),
            pl.BlockSpec((1, D_MODEL), full),
            pl.BlockSpec((1, D_MODEL), full),
            pl.BlockSpec((1, 1, D_MODEL), per_batch),
            pl.BlockSpec((1, 1, D_MODEL), per_batch),
            pl.BlockSpec((D_MODEL, LANES), full),
            pl.BlockSpec((1, LANES), full)],
        out_specs=[pl.BlockSpec((tm, D_MODEL), row),
                   pl.BlockSpec((tm * SLABS, LANES), row),
                   pl.BlockSpec((tm, LANES), row)],
        out_shape=[jax.ShapeDtypeStruct((TOKENS, D_MODEL), F32),
                   jax.ShapeDtypeStruct((TOKENS * SLABS, LANES), U32),
                   jax.ShapeDtypeStruct((TOKENS, LANES), F32)],
        scratch_shapes=scratch,
        compiler_params=_cparams(("arbitrary",)),
        name="proj_merge" if merge else "proj",
    )(*mix_in, w_bf, x2, gate, ln_g, ln_b, scale, shift, wr, br)


def _slab_rows(row):
    if isinstance(row, int):
        return pl.ds(row * SLABS, SLABS)
    return pl.ds(pl.multiple_of(row * SLABS, SLABS), SLABS)


def _row_copy(src, dst, s_row, d_row, sem):
    return pltpu.make_async_copy(src.at[_slab_rows(s_row)], dst.at[_slab_rows(d_row)], sem)


def _moe_kernel(be_ref, nv_ref, rt_ref, hs_hbm, wg_ref, wu_ref, wd_ref, ys_ref, xbuf, wg_bf, wu_bf, wd_bf, sems):
    i = pl.program_id(0)
    n_blk = pl.num_programs(0)
    nv = nv_ref[i]

    def gather(blk, slot):
        def issue(k, carry):
            for p in range(2):
                row = 2 * k + p
                _row_copy(hs_hbm, xbuf, rt_ref[blk * MOE_BLK + row], slot * MOE_BLK + row,
                          sems.at[slot]).start(priority=p)
            return carry
        lax.fori_loop(0, MOE_BLK // 2, issue, 0, unroll=4)

    @pl.when(i == 0)
    def _():
        gather(0, 0)

    nxt = jnp.minimum(i + 1, n_blk - 1)

    @pl.when((i + 1 < n_blk) & (nv_ref[nxt] > 0))
    def _():
        gather(nxt, nxt % 2)

    @pl.when(nv > 0)
    def _():
        prev = be_ref[jnp.maximum(i - 1, 0)]

        @pl.when((i == 0) | (be_ref[i] != prev))
        def _():
            wg_bf[...] = wg_ref[0, 0].astype(BF16)
            wu_bf[...] = wu_ref[0, 0].astype(BF16)
            wd_bf[...] = wd_ref[0, 0].astype(BF16)

        slot = i % 2

        def drain(k, carry):
            _row_copy(hs_hbm, xbuf, 0, 0, sems.at[slot]).wait()
            return carry
        lax.fori_loop(0, MOE_BLK, drain, 0, unroll=8)

        x = _load_slabs(xbuf, pl.multiple_of(slot * (MOE_BLK * SLABS), MOE_BLK * SLABS), MOE_BLK)
        rows = lax.broadcasted_iota(jnp.int32, (MOE_BLK, 1), 0)
        x = jnp.where(rows < nv, x, 0.0).astype(BF16)
        g = jnp.dot(x, wg_bf[...], preferred_element_type=F32)
        u = jnp.dot(x, wu_bf[...], preferred_element_type=F32)
        a = (g * (1.0 / (1.0 + jnp.exp(-g)))) * u
        y = jnp.dot(a.astype(BF16), wd_bf[...], preferred_element_type=F32)
        _store_slabs(ys_ref, y)

    @pl.when(nv <= 0)
    def _():
        ys_ref[...] = jnp.zeros_like(ys_ref)


def _moe_ffn(block_e, block_nv, row_tok, hs, w_gate, w_up, w_down, layer):
    wmap = lambda i, be, nv, rt: (layer, be[i], 0, 0)
    return pl.pallas_call(
        _moe_kernel,
        grid_spec=pltpu.PrefetchScalarGridSpec(
            num_scalar_prefetch=3,
            grid=(MOE_NB,),
            in_specs=[pl.BlockSpec(memory_space=pl.ANY),
                      pl.BlockSpec((1, 1, D_MODEL, EXPERT_DIM), wmap),
                      pl.BlockSpec((1, 1, D_MODEL, EXPERT_DIM), wmap),
                      pl.BlockSpec((1, 1, EXPERT_DIM, D_MODEL), wmap)],
            out_specs=pl.BlockSpec((MOE_BLK * SLABS, LANES), lambda i, be, nv, rt: (i, 0)),
            scratch_shapes=[pltpu.VMEM((2 * MOE_BLK * SLABS, LANES), U32),
                            pltpu.VMEM((D_MODEL, EXPERT_DIM), BF16),
                            pltpu.VMEM((D_MODEL, EXPERT_DIM), BF16),
                            pltpu.VMEM((EXPERT_DIM, D_MODEL), BF16),
                            pltpu.SemaphoreType.DMA((2,))]),
        out_shape=jax.ShapeDtypeStruct((MOE_ROWS * SLABS, LANES), U32),
        compiler_params=pltpu.CompilerParams(dimension_semantics=("arbitrary",), vmem_limit_bytes=VMEM_LIMIT,
                                             disable_bounds_checks=True),
        name="moe_ffn",
    )(block_e, block_nv, row_tok, hs, w_gate, w_up, w_down)


COMB_TM = 256


def _combine_kernel(dest_ref, ys_hbm, gt_ref, x_ref, gate_ref, g_ref, b_ref, *rest, modulate):
    if modulate:
        sc_ref, sh_ref, xn_ref, hn_ref, buf, sem = rest
    else:
        xn_ref, buf, sem = rest
    i = pl.program_id(0)
    tm = COMB_TM

    def copy(src_row, slot):
        return _row_copy(ys_hbm, buf, src_row, slot, sem)

    def issue(t, carry):
        tok = i * tm + t
        copy(dest_ref[2 * tok], t).start(priority=0)
        copy(dest_ref[2 * tok + 1], tm + t).start(priority=1)
        return carry

    lax.fori_loop(0, tm, issue, 0)

    def drain(t, carry):
        copy(0, 0).wait()
        copy(0, 0).wait()
        return carry

    lax.fori_loop(0, tm, drain, 0)

    gt = gt_ref[...]
    y = _load_slabs(buf, 0, tm) * gt[:, 0:1] + _load_slabs(buf, tm * SLABS, tm) * gt[:, 1:2]
    xn = _deepnorm(x_ref[...], y, gate_ref[0], g_ref[...], b_ref[...])
    xn_ref[...] = xn
    if modulate:
        hn_ref[...] = (xn * (1.0 + sc_ref[0]) + sh_ref[0]).astype(BF16)


def _combine(dest, ys, gates, x2, gate, ln_g, ln_b, scale=None, shift=None):
    tm = COMB_TM
    per_b = SEQ // tm
    modulate = scale is not None
    row = lambda i, d: (i, 0)
    full = lambda i, d: (0, 0)
    per_batch = lambda i, d: (i // per_b, 0, 0)
    in_specs = [pl.BlockSpec(memory_space=pl.ANY),
                pl.BlockSpec((tm, 2), row),
                pl.BlockSpec((tm, D_MODEL), row),
                pl.BlockSpec((1, 1, D_MODEL), per_batch),
                pl.BlockSpec((1, D_MODEL), full),
                pl.BlockSpec((1, D_MODEL), full)]
    out_specs = [pl.BlockSpec((tm, D_MODEL), row)]
    out_shape = [jax.ShapeDtypeStruct((TOKENS, D_MODEL), F32)]
    args = [dest, ys, gates, x2, gate, ln_g, ln_b]
    if modulate:
        in_specs += [pl.BlockSpec((1, 1, D_MODEL), per_batch)] * 2
        out_specs.append(pl.BlockSpec((tm, D_MODEL), row))
        out_shape.append(jax.ShapeDtypeStruct((TOKENS, D_MODEL), BF16))
        args += [scale, shift]
    return pl.pallas_call(
        functools.partial(_combine_kernel, modulate=modulate),
        grid_spec=pltpu.PrefetchScalarGridSpec(
            num_scalar_prefetch=1,
            grid=(TOKENS // tm,),
            in_specs=in_specs,
            out_specs=out_specs,
            scratch_shapes=[pltpu.VMEM((2 * tm * SLABS, LANES), U32), pltpu.SemaphoreType.DMA(())]),
        out_shape=out_shape,
        compiler_params=_cparams(("arbitrary",)),
        name="moe_combine",
    )(*args)


def _route(logits):
    lc = logits[:, :N_GROUPS]
    lf = logits[:, N_GROUPS:N_GROUPS + N_EXPERTS].reshape(TOKENS, N_GROUPS, EPG)
    pc = jax.nn.softmax(lc, axis=-1)
    pg = jnp.max(pc, axis=-1, keepdims=True)
    gi = jnp.argmax(pc, axis=-1)[:, None]
    lf_sel = jnp.take_along_axis(lf, gi[:, :, None], axis=1)[:, 0]
    pf = jax.nn.softmax(lf_sel, axis=-1)
    e1 = jnp.argmax(pf, axis=-1)[:, None]
    p1 = jnp.max(pf, axis=-1, keepdims=True)
    pf2 = jnp.where(jnp.arange(EPG)[None, :] == e1, -1.0, pf)
    e2 = jnp.argmax(pf2, axis=-1)[:, None]
    p2 = jnp.max(pf2, axis=-1, keepdims=True)
    pk = jnp.concatenate([p1, p2], axis=1)
    ek = jnp.concatenate([e1, e2], axis=1)
    gates = pg * (pk / jnp.sum(pk, axis=-1, keepdims=True))
    e_flat = (gi * EPG + ek).reshape(-1).astype(jnp.int32)
    onehot = (e_flat[:, None] == jnp.arange(N_EXPERTS, dtype=jnp.int32)[None, :]).astype(jnp.int32)
    csum = jnp.cumsum(onehot, axis=0)
    rank = jnp.sum((csum - onehot) * onehot, axis=1)
    counts = csum[-1]
    padded = (counts + MOE_BLK - 1) // MOE_BLK * MOE_BLK
    pends = jnp.cumsum(padded)
    pstarts = pends - padded
    dest = (pstarts[e_flat] + rank).astype(jnp.int32)
    blk_start = jnp.arange(MOE_NB, dtype=jnp.int32) * MOE_BLK
    block_e = jnp.clip(jnp.searchsorted(pends, blk_start, side='right'), 0, N_EXPERTS - 1).astype(jnp.int32)
    block_nv = jnp.clip(counts[block_e] - (blk_start - pstarts[block_e]), 0, MOE_BLK)
    block_nv = jnp.where(blk_start < pends[-1], block_nv, 0).astype(jnp.int32)
    row_tok = jnp.zeros((MOE_ROWS,), jnp.int32).at[dest].set(jnp.arange(2 * TOKENS, dtype=jnp.int32) // 2)
    return gates, dest, row_tok, block_e, block_nv


def _moe(hs, logits, x2, gate, ln_g, ln_b, w_gate, w_up, w_down, layer, scale=None, shift=None):
    gates, dest, row_tok, block_e, block_nv = _route(logits)
    ys = _moe_ffn(block_e, block_nv, row_tok, hs, w_gate, w_up, w_down, layer)
    return _combine(dest, ys, gates, x2, gate, ln_g, ln_b, scale, shift)


FN_TM = 256


def _fnet_in_kernel(h_ref, w_ref, cs_ref, ab_ref):
    u = jnp.dot(h_ref[...], w_ref[...], preferred_element_type=F32).astype(BF16)
    for g in range(FNET_GROUPS):
        sl = slice(g * FNET_CH, (g + 1) * FNET_CH)
        ab = jnp.dot(u[:, sl], cs_ref[...], preferred_element_type=F32)
        ab_ref[0, 0, :, sl] = ab[:, :FNET_CH].astype(BF16)
        ab_ref[0, 1, :, sl] = ab[:, FNET_CH:].astype(BF16)


def _fnet_in(h_bf, w_in_bf, cs):
    tm = FN_TM
    per_b = SEQ // tm
    return pl.pallas_call(
        _fnet_in_kernel,
        grid=(TOKENS // tm,),
        in_specs=[pl.BlockSpec((tm, D_MODEL), lambda i: (i, 0)),
                  pl.BlockSpec((D_MODEL, D_MODEL), lambda i: (0, 0)),
                  pl.BlockSpec((FNET_CH, 2 * FNET_CH), lambda i: (0, 0))],
        out_specs=pl.BlockSpec((1, 2, tm, D_MODEL), lambda i: (i // per_b, 0, i % per_b, 0)),
        out_shape=jax.ShapeDtypeStruct((BATCH, 2, SEQ, D_MODEL), BF16),
        compiler_params=_cparams(("arbitrary",)),
        name="fnet_in",
    )(h_bf, w_in_bf, cs)


DFT_R = 4
DFT_Q = SEQ // DFT_R
DFT_TN = 256
DFT_CHUNK = 256
_C4 = (1, 0, -1, 0)
_S4 = (0, 1, 0, -1)


def _signed_sum(terms):
    acc = None
    for sign, v in terms:
        if sign == 0:
            continue
        if acc is None:
            acc = v if sign > 0 else -v
        else:
            acc = acc + v if sign > 0 else acc - v
    return acc


def _dft_kernel(a_ref, b_ref, m_hbm, y_ref, m_scr, ab_scr, y_scr, sem):
    @pl.when((pl.program_id(0) == 0) & (pl.program_id(1) == 0))
    def _():
        cp = pltpu.make_async_copy(m_hbm, m_scr, sem)
        cp.start()
        cp.wait()

    for ch in range(DFT_Q // DFT_CHUNK):
        rows = [slice(m * DFT_Q + ch * DFT_CHUNK, m * DFT_Q + (ch + 1) * DFT_CHUNK) for m in range(DFT_R)]
        a = [a_ref[0, 0, r, :].astype(F32) for r in rows]
        b = [b_ref[0, 0, r, :].astype(F32) for r in rows]
        for rho in range(DFT_R):
            q = [(rho * m) % 4 for m in range(DFT_R)]
            ap = _signed_sum([(_C4[q[m]], a[m]) for m in range(DFT_R)] + [(-_S4[q[m]], b[m]) for m in range(DFT_R)])
            bp = _signed_sum([(_S4[q[m]], a[m]) for m in range(DFT_R)] + [(_C4[q[m]], b[m]) for m in range(DFT_R)])
            ab_scr[rho, ch * DFT_CHUNK:(ch + 1) * DFT_CHUNK, :] = ap.astype(BF16)
            ab_scr[rho, DFT_Q + ch * DFT_CHUNK:DFT_Q + (ch + 1) * DFT_CHUNK, :] = bp.astype(BF16)

    for rho in range(DFT_R):
        y = jnp.dot(m_scr[rho], ab_scr[rho], preferred_element_type=F32)
        for c in range(DFT_TN // LANES):
            y_scr[c, pl.ds(rho, DFT_Q, stride=DFT_R), :] = y[:, c * LANES:(c + 1) * LANES]
    for c in range(DFT_TN // LANES):
        y_ref[0, :, c * LANES:(c + 1) * LANES] = y_scr[c].astype(BF16)


def _seq_dft(mats, ab):
    tn = DFT_TN
    return pl.pallas_call(
        _dft_kernel,
        grid=(BATCH, D_MODEL // tn),
        in_specs=[pl.BlockSpec((1, 1, SEQ, tn), lambda b, j: (b, 0, 0, j)),
                  pl.BlockSpec((1, 1, SEQ, tn), lambda b, j: (b, 1, 0, j)),
                  pl.BlockSpec(memory_space=pl.ANY)],
        out_specs=pl.BlockSpec((1, SEQ, tn), lambda b, j: (b, 0, j)),
        out_shape=jax.ShapeDtypeStruct((BATCH, SEQ, D_MODEL), BF16),
        scratch_shapes=[pltpu.VMEM((DFT_R, DFT_Q, 2 * DFT_Q), BF16),
                        pltpu.VMEM((DFT_R, 2 * DFT_Q, tn), BF16),
                        pltpu.VMEM((tn // LANES, SEQ, LANES), F32),
                        pltpu.SemaphoreType.DMA(())],
        compiler_params=_cparams(("arbitrary", "arbitrary")),
        name="seq_dft",
    )(ab, ab, mats)


def _dft_constants():
    n = np.arange(FNET_CH)
    ang = 2.0 * np.pi * ((n[:, None] * n[None, :]) % FNET_CH) / FNET_CH
    cs = np.concatenate([np.cos(ang), np.sin(ang)], axis=1) / math.sqrt(FNET_CH)
    return jnp.asarray(cs, dtype=BF16)


def _seq_dft_matrices():
    shape = (DFT_R, DFT_Q, DFT_Q)
    rho = lax.broadcasted_iota(jnp.int32, shape, 0)
    kp = lax.broadcasted_iota(jnp.int32, shape, 1)
    jp = lax.broadcasted_iota(jnp.int32, shape, 2)
    ang = (((DFT_R * kp + rho) * jp) & (SEQ - 1)).astype(F32) * (2.0 * math.pi / SEQ)
    norm = 1.0 / math.sqrt(SEQ)
    return jnp.concatenate([(jnp.cos(ang) * norm).astype(BF16), (jnp.sin(ang) * (-norm)).astype(BF16)], axis=2)


def kernel(x, c, positions, ada_w, ada_b, attn_w_qkv, attn_w_o, fnet_w_in, fnet_w_out, ln_g, ln_b,
           router_coarse_w, router_coarse_b, router_fine_w, router_fine_b,
           expert_w_gate, expert_w_up, expert_w_down):
    x2 = x.reshape(TOKENS, D_MODEL)

    c8 = jnp.zeros((8, D_MODEL), F32).at[:BATCH].set(c)
    mod = _adaln(c8, ada_w.reshape(4, D_MODEL, 3 * D_MODEL), ada_b.reshape(4, 1, 3 * D_MODEL))[:, :BATCH]

    def mod_rows(idx):
        m = mod[idx].reshape(BATCH, 3, 1, D_MODEL)
        return m[:, 0], m[:, 1], m[:, 2]

    def router_params(i):
        wr = jnp.zeros((D_MODEL, LANES), F32)
        wr = wr.at[:, :N_GROUPS].set(router_coarse_w[i]).at[:, N_GROUPS:N_GROUPS + N_EXPERTS].set(router_fine_w[i])
        br = jnp.zeros((1, LANES), F32)
        br = br.at[0, :N_GROUPS].set(router_coarse_b[i]).at[0, N_GROUPS:N_GROUPS + N_EXPERTS].set(router_fine_b[i])
        return wr, br

    inv_freq = np.float32(ROPE_THETA) ** (-np.arange(0, ROT_DIM, 2, dtype=np.float32) / np.float32(ROT_DIM))
    invf = np.zeros((1, LANES), np.float32)
    invf[0, :ROT_HALF] = inv_freq
    invf[0, ROT_HALF:ROT_DIM] = inv_freq
    tabs = _rot_tables(positions.reshape(TOKENS, 1), jnp.asarray(invf))

    shift, scale, gate = mod_rows(0)
    outs, lses = [], []
    for g, dil in enumerate(DILATIONS):
        o_g, lse_g = _attention_group(_qkv(x2, scale, shift, attn_w_qkv, tabs, g, dil), dil)
        outs.append(o_g)
        lses.append(lse_g)
    shift2, scale2, gate2 = mod_rows(1)
    wr, br = router_params(0)
    x2, hs, logits = _proj_ln_router(outs + lses, attn_w_o[0].astype(BF16), x2, gate, ln_g[0, 0][None], ln_b[0, 0][None],
                                     scale2, shift2, wr, br, merge=True)
    shift3, scale3, gate3 = mod_rows(2)
    x2, h_bf = _moe(hs, logits, x2, gate2, ln_g[0, 1][None], ln_b[0, 1][None],
                    expert_w_gate, expert_w_up, expert_w_down, 0, scale3, shift3)

    ab = _fnet_in(h_bf, fnet_w_in[0].astype(BF16), _dft_constants())
    y = _seq_dft(_seq_dft_matrices(), ab).reshape(TOKENS, D_MODEL)
    shift4, scale4, gate4 = mod_rows(3)
    wr, br = router_params(1)
    x2, hs, logits = _proj_ln_router([y], fnet_w_out[0].astype(BF16), x2, gate3, ln_g[1, 0][None], ln_b[1, 0][None],
                                     scale4, shift4, wr, br, merge=False)
    (x2,) = _moe(hs, logits, x2, gate4, ln_g[1, 1][None], ln_b[1, 1][None],
                 expert_w_gate, expert_w_up, expert_w_down, 1)
    return x2.reshape(BATCH, SEQ, D_MODEL)
```

```python
import functools
import math

import numpy as np
import jax
import jax.numpy as jnp
from jax import lax
from jax.experimental import pallas as pl
from jax.experimental.pallas import tpu as pltpu

F32 = jnp.float32
BF16 = jnp.bfloat16
U32 = jnp.uint32

D_MODEL = 2048
BATCH = 2
SEQ = 4096
TOKENS = BATCH * SEQ
DEPTH = 2
HEAD_DIM = 128
HEADS = D_MODEL // HEAD_DIM
DILATIONS = (1, 4, 16)
RADIUS = 64
N_DIL = 3
QKV_COLS = N_DIL * 3 * D_MODEL
ROT_DIM = HEAD_DIM // 4
ROT_HALF = ROT_DIM // 2
ROPE_THETA = 500000.0
FNET_GROUPS = 4
FNET_CH = D_MODEL // FNET_GROUPS
N_GROUPS = 4
EPG = 8
N_EXPERTS = N_GROUPS * EPG
EXPERT_DIM = D_MODEL // 4
LN_EPS = 1e-5
NEG_INF = -1e30
ALPHA = (2 * DEPTH) ** 0.25

LANES = 128
SLABS = D_MODEL // (2 * LANES)
VMEM_LIMIT = 56 * 1024 * 1024

MOE_BLK = 256
MOE_NB = (2 * TOKENS) // MOE_BLK + N_EXPERTS
MOE_ROWS = MOE_NB * MOE_BLK


def _cparams(sem):
    return pltpu.CompilerParams(dimension_semantics=sem, vmem_limit_bytes=VMEM_LIMIT)


def _adaln_kernel(c_ref, w_ref, b_ref, o_ref):
    c = c_ref[...]
    sc = (c * (1.0 / (1.0 + jnp.exp(-c)))).astype(BF16)
    m = jnp.dot(sc, w_ref[0].astype(BF16), preferred_element_type=F32)
    o_ref[0] = m + b_ref[0]


def _adaln(c8, w4, b4):
    tn = 768
    n = 3 * D_MODEL
    return pl.pallas_call(
        _adaln_kernel,
        grid=(4, n // tn),
        in_specs=[pl.BlockSpec((8, D_MODEL), lambda s, j: (0, 0)),
                  pl.BlockSpec((1, D_MODEL, tn), lambda s, j: (s, 0, j)),
                  pl.BlockSpec((1, 1, tn), lambda s, j: (s, 0, j))],
        out_specs=pl.BlockSpec((1, 8, tn), lambda s, j: (s, 0, j)),
        out_shape=jax.ShapeDtypeStruct((4, 8, n), F32),
        compiler_params=_cparams(("arbitrary", "arbitrary")),
        name="adaln",
    )(c8, w4, b4)


def _rot_tab_kernel(pos_ref, invf_ref, o_ref):
    ang = pos_ref[...].astype(F32) * invf_ref[...]
    lane = lax.broadcasted_iota(jnp.int32, ang.shape, 1)
    c = jnp.where(lane < ROT_DIM, jnp.cos(ang), 1.0)
    s = jnp.sin(ang)
    s_lo = jnp.where(lane < ROT_HALF, -s, 0.0)
    s_hi = jnp.where((lane >= ROT_HALF) & (lane < ROT_DIM), s, 0.0)
    qs = HEAD_DIM ** -0.5
    o_ref[0] = c * qs
    o_ref[1] = s_lo * qs
    o_ref[2] = s_hi * qs
    o_ref[3] = c
    o_ref[4] = s_lo
    o_ref[5] = s_hi
    o_ref[6] = jnp.ones_like(c)
    o_ref[7] = jnp.zeros_like(c)
    o_ref[8] = jnp.zeros_like(c)


def _rot_tables(pos_col, invf):
    tm = 1024
    n = pos_col.shape[0]
    return pl.pallas_call(
        _rot_tab_kernel,
        grid=(n // tm,),
        in_specs=[pl.BlockSpec((tm, 1), lambda i: (i, 0)),
                  pl.BlockSpec((1, LANES), lambda i: (0, 0))],
        out_specs=pl.BlockSpec((9, tm, LANES), lambda i: (0, i, 0)),
        out_shape=jax.ShapeDtypeStruct((9, n, LANES), F32),
        compiler_params=_cparams(("arbitrary",)),
        name="rot_tables",
    )(pos_col, invf)


QKV_TM = 1024
QKV_TN = 512


def _rotate(t, c, s_lo, s_hi):
    return (t * c + pltpu.roll(t, LANES - ROT_HALF, axis=1) * s_lo
            + pltpu.roll(t, ROT_HALF, axis=1) * s_hi)


QKV_NI = TOKENS // QKV_TM
QKV_NJ = 3 * D_MODEL // QKV_TN
QKV_TILES = QKV_NI * QKV_NJ


def _qkv_kernel(x_ref, sc_ref, sh_ref, w_ref, tab_ref, o_ref, h_ref, acc_a, acc_b, *, dil):
    g = pl.program_id(0)

    @pl.when(g % QKV_NJ == 0)
    def _():
        h_ref[...] = (x_ref[...] * (1.0 + sc_ref[0]) + sh_ref[0]).astype(BF16)

    @pl.when(g == 0)
    def _():
        acc_b[...] = jnp.zeros_like(acc_b)

    rows = QKV_TM // dil

    def step(read_ref, write_ref):
        acc = jnp.dot(h_ref[...], w_ref[0].astype(BF16), preferred_element_type=F32)
        for hc in range(QKV_TN // LANES):
            write_ref[hc] = acc[:, hc * LANES:(hc + 1) * LANES]
        for hc in range(QKV_TN // LANES):
            sl = slice(hc * LANES, (hc + 1) * LANES)
            for r in range(dil):
                rs = slice(r * rows, (r + 1) * rows)
                t = read_ref[hc] if dil == 1 else read_ref[hc, pl.ds(r, rows, stride=dil), :]
                o_ref[0, r, :, sl] = _rotate(t, tab_ref[0, rs], tab_ref[1, rs], tab_ref[2, rs]).astype(BF16)

    @pl.when(g % 2 == 0)
    def _():
        step(acc_b, acc_a)

    @pl.when(g % 2 == 1)
    def _():
        step(acc_a, acc_b)


def _qkv(x2, scale, shift, w_qkv, tabs, grp, dil):
    tm, tn = QKV_TM, QKV_TN
    per_b = SEQ // tm

    def cur(g):
        t = jnp.minimum(g, QKV_TILES - 1)
        return t // QKV_NJ, t % QKV_NJ

    def prv(g):
        t = jnp.maximum(g - 1, 0)
        return t // QKV_NJ, t % QKV_NJ

    return pl.pallas_call(
        functools.partial(_qkv_kernel, dil=dil),
        grid=(QKV_TILES + 1,),
        in_specs=[pl.BlockSpec((tm, D_MODEL), lambda g: (cur(g)[0], 0)),
                  pl.BlockSpec((1, 1, D_MODEL), lambda g: (cur(g)[0] // per_b, 0, 0)),
                  pl.BlockSpec((1, 1, D_MODEL), lambda g: (cur(g)[0] // per_b, 0, 0)),
                  pl.BlockSpec((1, D_MODEL, tn), lambda g: (0, 0, grp * QKV_NJ + cur(g)[1])),
                  pl.BlockSpec((3, tm, LANES),
                               lambda g: (prv(g)[1] // (D_MODEL // tn), grp * QKV_NI + prv(g)[0], 0))],
        out_specs=pl.BlockSpec((1, dil, tm // dil, tn),
                               lambda g: (prv(g)[0] // per_b, 0, prv(g)[0] % per_b, prv(g)[1])),
        out_shape=jax.ShapeDtypeStruct((BATCH, dil, SEQ // dil, 3 * D_MODEL), BF16),
        scratch_shapes=[pltpu.VMEM((tm, D_MODEL), BF16),
                        pltpu.VMEM((tn // LANES, tm, LANES), F32),
                        pltpu.VMEM((tn // LANES, tm, LANES), F32)],
        compiler_params=_cparams(("arbitrary",)),
        name=f"qkv_proj_dil{dil}",
    )(x2, scale, shift, w_qkv, tabs)


ATT_TQ = 128
ATT_TK = ATT_TQ + 2 * RADIUS


def _attn_kernel(q_ref, kp_ref, kc_ref, kn_ref, vp_ref, vc_ref, vn_ref, o_ref, lse_ref, kbuf, vbuf, *, seq_len):
    i = pl.program_id(2)
    kbuf[0:RADIUS] = kp_ref[0, 0]
    kbuf[RADIUS:RADIUS + ATT_TQ] = kc_ref[0, 0]
    kbuf[RADIUS + ATT_TQ:ATT_TK] = kn_ref[0, 0]
    vbuf[0:RADIUS] = vp_ref[0, 0]
    vbuf[RADIUS:RADIUS + ATT_TQ] = vc_ref[0, 0]
    vbuf[RADIUS + ATT_TQ:ATT_TK] = vn_ref[0, 0]

    qpos = i * ATT_TQ + lax.broadcasted_iota(jnp.int32, (ATT_TQ, ATT_TK), 0)
    kpos = i * ATT_TQ - RADIUS + lax.broadcasted_iota(jnp.int32, (ATT_TQ, ATT_TK), 1)
    valid = (jnp.abs(qpos - kpos) <= RADIUS) & (kpos >= 0) & (kpos < seq_len)
    bias = jnp.where(valid, 0.0, NEG_INF).astype(F32)
    lane = lax.broadcasted_iota(jnp.int32, (ATT_TQ, LANES), 1)
    lse_all = jnp.zeros((ATT_TQ, LANES), F32)
    for h in range(HEADS):
        sl = slice(h * HEAD_DIM, (h + 1) * HEAD_DIM)
        s = lax.dot_general(q_ref[0, 0, :, sl], kbuf[:, sl], (((1,), (1,)), ((), ())),
                            preferred_element_type=F32) + bias
        m = jnp.max(s, axis=-1, keepdims=True)
        p = jnp.exp(s - m)
        den = jnp.sum(p, axis=-1, keepdims=True)
        o = jnp.dot(p.astype(BF16), vbuf[:, sl], preferred_element_type=F32) / den
        o_ref[0, 0, :, sl] = o.astype(BF16)
        lse_all = jnp.where(lane == h, m + jnp.log(den), lse_all)
    lse_ref[0, 0] = lse_all


def _attention_group(qkv, dil):
    seq_len = SEQ // dil
    tq = ATT_TQ
    sub = tq // RADIUS
    n_sub = seq_len // RADIUS

    def cur(which):
        return pl.BlockSpec((1, 1, tq, D_MODEL), lambda b, r, i: (b, r, i, which))

    def prev(which):
        return pl.BlockSpec((1, 1, RADIUS, D_MODEL), lambda b, r, i: (b, r, jnp.maximum(i * sub - 1, 0), which))

    def nxt(which):
        return pl.BlockSpec((1, 1, RADIUS, D_MODEL),
                            lambda b, r, i: (b, r, jnp.minimum((i + 1) * sub, n_sub - 1), which))

    return pl.pallas_call(
        functools.partial(_attn_kernel, seq_len=seq_len),
        grid=(BATCH, dil, seq_len // tq),
        in_specs=[cur(0), prev(1), cur(1), nxt(1), prev(2), cur(2), nxt(2)],
        out_specs=[pl.BlockSpec((1, 1, tq, D_MODEL), lambda b, r, i: (b, r, i, 0)),
                   pl.BlockSpec((1, 1, tq, LANES), lambda b, r, i: (b, r, i, 0))],
        out_shape=[jax.ShapeDtypeStruct((BATCH, dil, seq_len, D_MODEL), BF16),
                   jax.ShapeDtypeStruct((BATCH, dil, seq_len, LANES), F32)],
        scratch_shapes=[pltpu.VMEM((ATT_TK, D_MODEL), BF16), pltpu.VMEM((ATT_TK, D_MODEL), BF16)],
        compiler_params=_cparams(("arbitrary", "arbitrary", "arbitrary")),
        name=f"attn_dil{dil}",
    )(qkv, qkv, qkv, qkv, qkv, qkv, qkv)


def _deepnorm(x, y, gate, g, b):
    z = ALPHA * x + gate * y
    mu = jnp.mean(z, axis=-1, keepdims=True)
    zc = z - mu
    var = jnp.mean(zc * zc, axis=-1, keepdims=True)
    return zc * lax.rsqrt(var + LN_EPS) * g + b


def _split_bf16(a):
    hi = a.astype(BF16)
    lo = (a - hi.astype(F32)).astype(BF16)
    return hi, lo


def _router_logits(h, wr, br):
    h_hi, h_lo = _split_bf16(h)
    w_hi, w_lo = _split_bf16(wr)
    acc = jnp.dot(h_hi, w_hi, preferred_element_type=F32)
    acc = acc + jnp.dot(h_hi, w_lo, preferred_element_type=F32)
    acc = acc + jnp.dot(h_lo, w_hi, preferred_element_type=F32)
    return acc + br


def _store_slabs(ref, val):
    tm = val.shape[0]
    for s in range(SLABS):
        lo = val[:, (2 * s) * LANES:(2 * s + 1) * LANES].astype(BF16).astype(F32)
        hi = val[:, (2 * s + 1) * LANES:(2 * s + 2) * LANES].astype(BF16).astype(F32)
        word = ((lax.bitcast_convert_type(lo, U32) >> 16)
                | (lax.bitcast_convert_type(hi, U32) & jnp.uint32(0xFFFF0000)))
        ref[pl.ds(s, tm, stride=SLABS), :] = word


def _load_slabs(ref, base, tm):
    parts = []
    for s in range(SLABS):
        word = ref[pl.ds(base + s, tm, stride=SLABS), :]
        parts.append(lax.bitcast_convert_type(word << 16, F32))
        parts.append(lax.bitcast_convert_type(word & jnp.uint32(0xFFFF0000), F32))
    return jnp.concatenate(parts, axis=1)


PROJ_TM = 256


def _proj_kernel(*refs, merge):
    if merge:
        (o0, o1, o2, l0, l1, l2, w_ref, x_ref, gate_ref, g_ref, b_ref, sc_ref, sh_ref, wr_ref, br_ref,
         xn_ref, hs_ref, lg_ref, on1, on2, ln1, ln2) = refs
        for src, dst, dil in ((o1, on1, DILATIONS[1]), (o2, on2, DILATIONS[2])):
            for r in range(dil):
                for h in range(HEADS):
                    dst[h, pl.ds(r, PROJ_TM // dil, stride=dil), :] = (
                        src[0, r, :, h * HEAD_DIM:(h + 1) * HEAD_DIM].astype(F32))
        for src, dst, dil in ((l1, ln1, DILATIONS[1]), (l2, ln2, DILATIONS[2])):
            for r in range(dil):
                dst[pl.ds(r, PROJ_TM // dil, stride=dil), :] = src[0, r]
        ls = [l0[0, 0], ln1[...], ln2[...]]
        mx = jnp.maximum(jnp.maximum(ls[0], ls[1]), ls[2])
        es = [jnp.exp(l - mx) for l in ls]
        inv = 1.0 / (es[0] + es[1] + es[2])
        ws = [e * inv for e in es]
        parts = []
        for h in range(HEADS):
            sl = slice(h * HEAD_DIM, (h + 1) * HEAD_DIM)
            acc = ws[0][:, h:h + 1] * o0[0, 0, :, sl].astype(F32)
            acc = acc + ws[1][:, h:h + 1] * on1[h]
            acc = acc + ws[2][:, h:h + 1] * on2[h]
            parts.append(acc.astype(BF16))
        y_in = jnp.concatenate(parts, axis=1)
    else:
        (y_ref, w_ref, x_ref, gate_ref, g_ref, b_ref, sc_ref, sh_ref, wr_ref, br_ref,
         xn_ref, hs_ref, lg_ref) = refs
        y_in = y_ref[...]
    y = jnp.dot(y_in, w_ref[...], preferred_element_type=F32)
    xn = _deepnorm(x_ref[...], y, gate_ref[0], g_ref[...], b_ref[...])
    xn_ref[...] = xn
    h = xn * (1.0 + sc_ref[0]) + sh_ref[0]
    _store_slabs(hs_ref, h)
    lg_ref[...] = _router_logits(h, wr_ref[...], br_ref[...])


def _proj_ln_router(mix_in, w_bf, x2, gate, ln_g, ln_b, scale, shift, wr, br, *, merge):
    tm = PROJ_TM
    per_b = SEQ // tm
    row = lambda i: (i, 0)
    full = lambda i: (0, 0)
    per_batch = lambda i: (i // per_b, 0, 0)
    scratch = []
    if merge:
        def resid(width, dil):
            return pl.BlockSpec((1, dil, tm // dil, width), lambda i: (i // per_b, 0, i % per_b, 0))
        mix_specs = ([resid(D_MODEL, dil) for dil in DILATIONS] + [resid(LANES, dil) for dil in DILATIONS])
        scratch = [pltpu.VMEM((HEADS, tm, HEAD_DIM), F32), pltpu.VMEM((HEADS, tm, HEAD_DIM), F32),
                   pltpu.VMEM((tm, LANES), F32), pltpu.VMEM((tm, LANES), F32)]
    else:
        mix_specs = [pl.BlockSpec((tm, D_MODEL), row)]
    return pl.pallas_call(
        functools.partial(_proj_kernel, merge=merge),
        grid=(TOKENS // tm,),
        in_specs=mix_specs + [
            pl.BlockSpec((D_MODEL, D_MODEL), full),
            pl.BlockSpec((tm, D_MODEL), row),
            pl.BlockSpec((1, 1, D_MODEL), per_batch),
            pl.BlockSpec((1, D_MODEL), full),
            pl.BlockSpec((1, D_MODEL), full),
            pl.BlockSpec((1, 1, D_MODEL), per_batch),
            pl.BlockSpec((1, 1, D_MODEL), per_batch),
            pl.BlockSpec((D_MODEL, LANES), full),
            pl.BlockSpec((1, LANES), full)],
        out_specs=[pl.BlockSpec((tm, D_MODEL), row),
                   pl.BlockSpec((tm * SLABS, LANES), row),
                   pl.BlockSpec((tm, LANES), row)],
        out_shape=[jax.ShapeDtypeStruct((TOKENS, D_MODEL), F32),
                   jax.ShapeDtypeStruct((TOKENS * SLABS, LANES), U32),
                   jax.ShapeDtypeStruct((TOKENS, LANES), F32)],
        scratch_shapes=scratch,
        compiler_params=_cparams(("arbitrary",)),
        name="proj_merge" if merge else "proj",
    )(*mix_in, w_bf, x2, gate, ln_g, ln_b, scale, shift, wr, br)


def _slab_rows(row):
    if isinstance(row, int):
        return pl.ds(row * SLABS, SLABS)
    return pl.ds(pl.multiple_of(row * SLABS, SLABS), SLABS)


def _row_copy(src, dst, s_row, d_row, sem):
    return pltpu.make_async_copy(src.at[_slab_rows(s_row)], dst.at[_slab_rows(d_row)], sem)


def _moe_kernel(be_ref, nv_ref, rt_ref, hs_hbm, wg_ref, wu_ref, wd_ref, ys_ref, xbuf, wg_bf, wu_bf, wd_bf, sems):
    i = pl.program_id(0)
    n_blk = pl.num_programs(0)
    nv = nv_ref[i]

    def gather(blk, slot):
        def issue(k, carry):
            for p in range(2):
                row = 2 * k + p
                _row_copy(hs_hbm, xbuf, rt_ref[blk * MOE_BLK + row], slot * MOE_BLK + row,
                          sems.at[slot]).start(priority=p)
            return carry
        lax.fori_loop(0, MOE_BLK // 2, issue, 0, unroll=4)

    @pl.when(i == 0)
    def _():
        gather(0, 0)

    nxt = jnp.minimum(i + 1, n_blk - 1)

    @pl.when((i + 1 < n_blk) & (nv_ref[nxt] > 0))
    def _():
        gather(nxt, nxt % 2)

    @pl.when(nv > 0)
    def _():
        prev = be_ref[jnp.maximum(i - 1, 0)]

        @pl.when((i == 0) | (be_ref[i] != prev))
        def _():
            wg_bf[...] = wg_ref[0, 0].astype(BF16)
            wu_bf[...] = wu_ref[0, 0].astype(BF16)
            wd_bf[...] = wd_ref[0, 0].astype(BF16)

        slot = i % 2

        def drain(k, carry):
            _row_copy(hs_hbm, xbuf, 0, 0, sems.at[slot]).wait()
            return carry
        lax.fori_loop(0, MOE_BLK, drain, 0, unroll=8)

        x = _load_slabs(xbuf, pl.multiple_of(slot * (MOE_BLK * SLABS), MOE_BLK * SLABS), MOE_BLK)
        rows = lax.broadcasted_iota(jnp.int32, (MOE_BLK, 1), 0)
        x = jnp.where(rows < nv, x, 0.0).astype(BF16)
        g = jnp.dot(x, wg_bf[...], preferred_element_type=F32)
        u = jnp.dot(x, wu_bf[...], preferred_element_type=F32)
        a = (g * (1.0 / (1.0 + jnp.exp(-g)))) * u
        y = jnp.dot(a.astype(BF16), wd_bf[...], preferred_element_type=F32)
        _store_slabs(ys_ref, y)

    @pl.when(nv <= 0)
    def _():
        ys_ref[...] = jnp.zeros_like(ys_ref)


def _moe_ffn(block_e, block_nv, row_tok, hs, w_gate, w_up, w_down, layer):
    wmap = lambda i, be, nv, rt: (layer, be[i], 0, 0)
    return pl.pallas_call(
        _moe_kernel,
        grid_spec=pltpu.PrefetchScalarGridSpec(
            num_scalar_prefetch=3,
            grid=(MOE_NB,),
            in_specs=[pl.BlockSpec(memory_space=pl.ANY),
                      pl.BlockSpec((1, 1, D_MODEL, EXPERT_DIM), wmap),
                      pl.BlockSpec((1, 1, D_MODEL, EXPERT_DIM), wmap),
                      pl.BlockSpec((1, 1, EXPERT_DIM, D_MODEL), wmap)],
            out_specs=pl.BlockSpec((MOE_BLK * SLABS, LANES), lambda i, be, nv, rt: (i, 0)),
            scratch_shapes=[pltpu.VMEM((2 * MOE_BLK * SLABS, LANES), U32),
                            pltpu.VMEM((D_MODEL, EXPERT_DIM), BF16),
                            pltpu.VMEM((D_MODEL, EXPERT_DIM), BF16),
                            pltpu.VMEM((EXPERT_DIM, D_MODEL), BF16),
                            pltpu.SemaphoreType.DMA((2,))]),
        out_shape=jax.ShapeDtypeStruct((MOE_ROWS * SLABS, LANES), U32),
        compiler_params=pltpu.CompilerParams(dimension_semantics=("arbitrary",), vmem_limit_bytes=VMEM_LIMIT,
                                             disable_bounds_checks=True),
        name="moe_ffn",
    )(block_e, block_nv, row_tok, hs, w_gate, w_up, w_down)


COMB_TM = 256


def _combine_kernel(dest_ref, ys_hbm, gt_ref, x_ref, gate_ref, g_ref, b_ref, *rest, modulate):
    if modulate:
        sc_ref, sh_ref, xn_ref, hn_ref, buf, sem = rest
    else:
        xn_ref, buf, sem = rest
    i = pl.program_id(0)
    tm = COMB_TM

    def copy(src_row, slot):
        return _row_copy(ys_hbm, buf, src_row, slot, sem)

    def issue(t, carry):
        tok = i * tm + t
        copy(dest_ref[2 * tok], t).start(priority=0)
        copy(dest_ref[2 * tok + 1], tm + t).start(priority=1)
        return carry

    lax.fori_loop(0, tm, issue, 0)

    def drain(t, carry):
        copy(0, 0).wait()
        copy(0, 0).wait()
        return carry

    lax.fori_loop(0, tm, drain, 0)

    gt = gt_ref[...]
    y = _load_slabs(buf, 0, tm) * gt[:, 0:1] + _load_slabs(buf, tm * SLABS, tm) * gt[:, 1:2]
    xn = _deepnorm(x_ref[...], y, gate_ref[0], g_ref[...], b_ref[...])
    xn_ref[...] = xn
    if modulate:
        hn_ref[...] = (xn * (1.0 + sc_ref[0]) + sh_ref[0]).astype(BF16)


def _combine(dest, ys, gates, x2, gate, ln_g, ln_b, scale=None, shift=None):
    tm = COMB_TM
    per_b = SEQ // tm
    modulate = scale is not None
    row = lambda i, d: (i, 0)
    full = lambda i, d: (0, 0)
    per_batch = lambda i, d: (i // per_b, 0, 0)
    in_specs = [pl.BlockSpec(memory_space=pl.ANY),
                pl.BlockSpec((tm, 2), row),
                pl.BlockSpec((tm, D_MODEL), row),
                pl.BlockSpec((1, 1, D_MODEL), per_batch),
                pl.BlockSpec((1, D_MODEL), full),
                pl.BlockSpec((1, D_MODEL), full)]
    out_specs = [pl.BlockSpec((tm, D_MODEL), row)]
    out_shape = [jax.ShapeDtypeStruct((TOKENS, D_MODEL), F32)]
    args = [dest, ys, gates, x2, gate, ln_g, ln_b]
    if modulate:
        in_specs += [pl.BlockSpec((1, 1, D_MODEL), per_batch)] * 2
        out_specs.append(pl.BlockSpec((tm, D_MODEL), row))
        out_shape.append(jax.ShapeDtypeStruct((TOKENS, D_MODEL), BF16))
        args += [scale, shift]
    return pl.pallas_call(
        functools.partial(_combine_kernel, modulate=modulate),
        grid_spec=pltpu.PrefetchScalarGridSpec(
            num_scalar_prefetch=1,
            grid=(TOKENS // tm,),
            in_specs=in_specs,
            out_specs=out_specs,
            scratch_shapes=[pltpu.VMEM((2 * tm * SLABS, LANES), U32), pltpu.SemaphoreType.DMA(())]),
        out_shape=out_shape,
        compiler_params=_cparams(("arbitrary",)),
        name="moe_combine",
    )(*args)


def _route(logits):
    lc = logits[:, :N_GROUPS]
    lf = logits[:, N_GROUPS:N_GROUPS + N_EXPERTS].reshape(TOKENS, N_GROUPS, EPG)
    pc = jax.nn.softmax(lc, axis=-1)
    pg = jnp.max(pc, axis=-1, keepdims=True)
    gi = jnp.argmax(pc, axis=-1)[:, None]
    lf_sel = jnp.take_along_axis(lf, gi[:, :, None], axis=1)[:, 0]
    pf = jax.nn.softmax(lf_sel, axis=-1)
    e1 = jnp.argmax(pf, axis=-1)[:, None]
    p1 = jnp.max(pf, axis=-1, keepdims=True)
    pf2 = jnp.where(jnp.arange(EPG)[None, :] == e1, -1.0, pf)
    e2 = jnp.argmax(pf2, axis=-1)[:, None]
    p2 = jnp.max(pf2, axis=-1, keepdims=True)
    pk = jnp.concatenate([p1, p2], axis=1)
    ek = jnp.concatenate([e1, e2], axis=1)
    gates = pg * (pk / jnp.sum(pk, axis=-1, keepdims=True))
    e_flat = (gi * EPG + ek).reshape(-1).astype(jnp.int32)
    onehot = (e_flat[:, None] == jnp.arange(N_EXPERTS, dtype=jnp.int32)[None, :]).astype(jnp.int32)
    csum = jnp.cumsum(onehot, axis=0)
    rank = jnp.sum((csum - onehot) * onehot, axis=1)
    counts = csum[-1]
    padded = (counts + MOE_BLK - 1) // MOE_BLK * MOE_BLK
    pends = jnp.cumsum(padded)
    pstarts = pends - padded
    dest = (pstarts[e_flat] + rank).astype(jnp.int32)
    blk_start = jnp.arange(MOE_NB, dtype=jnp.int32) * MOE_BLK
    block_e = jnp.clip(jnp.searchsorted(pends, blk_start, side='right'), 0, N_EXPERTS - 1).astype(jnp.int32)
    block_nv = jnp.clip(counts[block_e] - (blk_start - pstarts[block_e]), 0, MOE_BLK)
    block_nv = jnp.where(blk_start < pends[-1], block_nv, 0).astype(jnp.int32)
    row_tok = jnp.zeros((MOE_ROWS,), jnp.int32).at[dest].set(jnp.arange(2 * TOKENS, dtype=jnp.int32) // 2)
    return gates, dest, row_tok, block_e, block_nv


def _moe(hs, logits, x2, gate, ln_g, ln_b, w_gate, w_up, w_down, layer, scale=None, shift=None):
    gates, dest, row_tok, block_e, block_nv = _route(logits)
    ys = _moe_ffn(block_e, block_nv, row_tok, hs, w_gate, w_up, w_down, layer)
    return _combine(dest, ys, gates, x2, gate, ln_g, ln_b, scale, shift)


FN_TM = 256


def _fnet_in_kernel(h_ref, w_ref, cs_ref, ab_ref):
    u = jnp.dot(h_ref[...], w_ref[...], preferred_element_type=F32).astype(BF16)
    for g in range(FNET_GROUPS):
        sl = slice(g * FNET_CH, (g + 1) * FNET_CH)
        ab = jnp.dot(u[:, sl], cs_ref[...], preferred_element_type=F32)
        ab_ref[0, 0, :, sl] = ab[:, :FNET_CH].astype(BF16)
        ab_ref[0, 1, :, sl] = ab[:, FNET_CH:].astype(BF16)


def _fnet_in(h_bf, w_in_bf, cs):
    tm = FN_TM
    per_b = SEQ // tm
    return pl.pallas_call(
        _fnet_in_kernel,
        grid=(TOKENS // tm,),
        in_specs=[pl.BlockSpec((tm, D_MODEL), lambda i: (i, 0)),
                  pl.BlockSpec((D_MODEL, D_MODEL), lambda i: (0, 0)),
                  pl.BlockSpec((FNET_CH, 2 * FNET_CH), lambda i: (0, 0))],
        out_specs=pl.BlockSpec((1, 2, tm, D_MODEL), lambda i: (i // per_b, 0, i % per_b, 0)),
        out_shape=jax.ShapeDtypeStruct((BATCH, 2, SEQ, D_MODEL), BF16),
        compiler_params=_cparams(("arbitrary",)),
        name="fnet_in",
    )(h_bf, w_in_bf, cs)


DFT_R = 4
DFT_Q = SEQ // DFT_R
DFT_TN = 256
DFT_CHUNK = 256
_C4 = (1, 0, -1, 0)
_S4 = (0, 1, 0, -1)


def _signed_sum(terms):
    acc = None
    for sign, v in terms:
        if sign == 0:
            continue
        if acc is None:
            acc = v if sign > 0 else -v
        else:
            acc = acc + v if sign > 0 else acc - v
    return acc


def _dft_kernel(a_ref, b_ref, m_hbm, y_ref, m_scr, ab_scr, y_scr, sem):
    @pl.when((pl.program_id(0) == 0) & (pl.program_id(1) == 0))
    def _():
        cp = pltpu.make_async_copy(m_hbm, m_scr, sem)
        cp.start()
        cp.wait()

    for ch in range(DFT_Q // DFT_CHUNK):
        rows = [slice(m * DFT_Q + ch * DFT_CHUNK, m * DFT_Q + (ch + 1) * DFT_CHUNK) for m in range(DFT_R)]
        a = [a_ref[0, 0, r, :].astype(F32) for r in rows]
        b = [b_ref[0, 0, r, :].astype(F32) for r in rows]
        for rho in range(DFT_R):
            q = [(rho * m) % 4 for m in range(DFT_R)]
            ap = _signed_sum([(_C4[q[m]], a[m]) for m in range(DFT_R)] + [(-_S4[q[m]], b[m]) for m in range(DFT_R)])
            bp = _signed_sum([(_S4[q[m]], a[m]) for m in range(DFT_R)] + [(_C4[q[m]], b[m]) for m in range(DFT_R)])
            ab_scr[rho, ch * DFT_CHUNK:(ch + 1) * DFT_CHUNK, :] = ap.astype(BF16)
            ab_scr[rho, DFT_Q + ch * DFT_CHUNK:DFT_Q + (ch + 1) * DFT_CHUNK, :] = bp.astype(BF16)

    for rho in range(DFT_R):
        y = jnp.dot(m_scr[rho], ab_scr[rho], preferred_element_type=F32)
        for c in range(DFT_TN // LANES):
            y_scr[c, pl.ds(rho, DFT_Q, stride=DFT_R), :] = y[:, c * LANES:(c + 1) * LANES]
    for c in range(DFT_TN // LANES):
        y_ref[0, :, c * LANES:(c + 1) * LANES] = y_scr[c].astype(BF16)


def _seq_dft(mats, ab):
    tn = DFT_TN
    return pl.pallas_call(
        _dft_kernel,
        grid=(BATCH, D_MODEL // tn),
        in_specs=[pl.BlockSpec((1, 1, SEQ, tn), lambda b, j: (b, 0, 0, j)),
                  pl.BlockSpec((1, 1, SEQ, tn), lambda b, j: (b, 1, 0, j)),
                  pl.BlockSpec(memory_space=pl.ANY)],
        out_specs=pl.BlockSpec((1, SEQ, tn), lambda b, j: (b, 0, j)),
        out_shape=jax.ShapeDtypeStruct((BATCH, SEQ, D_MODEL), BF16),
        scratch_shapes=[pltpu.VMEM((DFT_R, DFT_Q, 2 * DFT_Q), BF16),
                        pltpu.VMEM((DFT_R, 2 * DFT_Q, tn), BF16),
                        pltpu.VMEM((tn // LANES, SEQ, LANES), F32),
                        pltpu.SemaphoreType.DMA(())],
        compiler_params=_cparams(("arbitrary", "arbitrary")),
        name="seq_dft",
    )(ab, ab, mats)


def _dft_constants():
    n = np.arange(FNET_CH)
    ang = 2.0 * np.pi * ((n[:, None] * n[None, :]) % FNET_CH) / FNET_CH
    cs = np.concatenate([np.cos(ang), np.sin(ang)], axis=1) / math.sqrt(FNET_CH)
    return jnp.asarray(cs, dtype=BF16)


def _seq_dft_matrices():
    shape = (DFT_R, DFT_Q, DFT_Q)
    rho = lax.broadcasted_iota(jnp.int32, shape, 0)
    kp = lax.broadcasted_iota(jnp.int32, shape, 1)
    jp = lax.broadcasted_iota(jnp.int32, shape, 2)
    ang = (((DFT_R * kp + rho) * jp) & (SEQ - 1)).astype(F32) * (2.0 * math.pi / SEQ)
    norm = 1.0 / math.sqrt(SEQ)
    return jnp.concatenate([(jnp.cos(ang) * norm).astype(BF16), (jnp.sin(ang) * (-norm)).astype(BF16)], axis=2)


def kernel(x, c, positions, ada_w, ada_b, attn_w_qkv, attn_w_o, fnet_w_in, fnet_w_out, ln_g, ln_b,
           router_coarse_w, router_coarse_b, router_fine_w, router_fine_b,
           expert_w_gate, expert_w_up, expert_w_down):
    x2 = x.reshape(TOKENS, D_MODEL)

    c8 = jnp.zeros((8, D_MODEL), F32).at[:BATCH].set(c)
    mod = _adaln(c8, ada_w.reshape(4, D_MODEL, 3 * D_MODEL), ada_b.reshape(4, 1, 3 * D_MODEL))[:, :BATCH]

    def mod_rows(idx):
        m = mod[idx].reshape(BATCH, 3, 1, D_MODEL)
        return m[:, 0], m[:, 1], m[:, 2]

    def router_params(i):
        wr = jnp.zeros((D_MODEL, LANES), F32)
        wr = wr.at[:, :N_GROUPS].set(router_coarse_w[i]).at[:, N_GROUPS:N_GROUPS + N_EXPERTS].set(router_fine_w[i])
        br = jnp.zeros((1, LANES), F32)
        br = br.at[0, :N_GROUPS].set(router_coarse_b[i]).at[0, N_GROUPS:N_GROUPS + N_EXPERTS].set(router_fine_b[i])
        return wr, br

    inv_freq = np.float32(ROPE_THETA) ** (-np.arange(0, ROT_DIM, 2, dtype=np.float32) / np.float32(ROT_DIM))
    invf = np.zeros((1, LANES), np.float32)
    invf[0, :ROT_HALF] = inv_freq
    invf[0, ROT_HALF:ROT_DIM] = inv_freq
    pos_rm = [positions.reshape(BATCH, SEQ // QKV_TM, QKV_TM // dil, dil).transpose(0, 1, 3, 2).reshape(TOKENS)
              for dil in DILATIONS]
    tabs = _rot_tables(jnp.concatenate(pos_rm).reshape(N_DIL * TOKENS, 1), jnp.asarray(invf))

    shift, scale, gate = mod_rows(0)
    outs, lses = [], []
    for g, dil in enumerate(DILATIONS):
        o_g, lse_g = _attention_group(_qkv(x2, scale, shift, attn_w_qkv, tabs, g, dil), dil)
        outs.append(o_g)
        lses.append(lse_g)
    shift2, scale2, gate2 = mod_rows(1)
    wr, br = router_params(0)
    x2, hs, logits = _proj_ln_router(outs + lses, attn_w_o[0].astype(BF16), x2, gate, ln_g[0, 0][None], ln_b[0, 0][None],
                                     scale2, shift2, wr, br, merge=True)
    shift3, scale3, gate3 = mod_rows(2)
    x2, h_bf = _moe(hs, logits, x2, gate2, ln_g[0, 1][None], ln_b[0, 1][None],
                    expert_w_gate, expert_w_up, expert_w_down, 0, scale3, shift3)

    ab = _fnet_in(h_bf, fnet_w_in[0].astype(BF16), _dft_constants())
    y = _seq_dft(_seq_dft_matrices(), ab).reshape(TOKENS, D_MODEL)
    shift4, scale4, gate4 = mod_rows(3)
    wr, br = router_params(1)
    x2, hs, logits = _proj_ln_router([y], fnet_w_out[0].astype(BF16), x2, gate3, ln_g[1, 0][None], ln_b[1, 0][None],
                                     scale4, shift4, wr, br, merge=False)
    (x2,) = _moe(hs, logits, x2, gate4, ln_g[1, 1][None], ln_b[1, 1][None],
                 expert_w_gate, expert_w_up, expert_w_down, 1)
    return x2.reshape(BATCH, SEQ, D_MODEL)
```

```python
import functools
import math

import numpy as np
import jax
import jax.numpy as jnp
from jax import lax
from jax.experimental import pallas as pl
from jax.experimental.pallas import tpu as pltpu

F32 = jnp.float32
BF16 = jnp.bfloat16
U32 = jnp.uint32

D_MODEL = 2048
BATCH = 2
SEQ = 4096
TOKENS = BATCH * SEQ
DEPTH = 2
HEAD_DIM = 128
HEADS = D_MODEL // HEAD_DIM
DILATIONS = (1, 4, 16)
RADIUS = 64
N_DIL = 3
QKV_COLS = N_DIL * 3 * D_MODEL
ROT_DIM = HEAD_DIM // 4
ROT_HALF = ROT_DIM // 2
ROPE_THETA = 500000.0
FNET_GROUPS = 4
FNET_CH = D_MODEL // FNET_GROUPS
N_GROUPS = 4
EPG = 8
N_EXPERTS = N_GROUPS * EPG
EXPERT_DIM = D_MODEL // 4
LN_EPS = 1e-5
NEG_INF = -1e30
ALPHA = (2 * DEPTH) ** 0.25

LANES = 128
SLABS = D_MODEL // (2 * LANES)
VMEM_LIMIT = 56 * 1024 * 1024

MOE_BLK = 256
MOE_NB = (2 * TOKENS) // MOE_BLK + N_EXPERTS
MOE_ROWS = MOE_NB * MOE_BLK


def _cparams(sem):
    return pltpu.CompilerParams(dimension_semantics=sem, vmem_limit_bytes=VMEM_LIMIT)


def _adaln_kernel(c_ref, w_ref, b_ref, o_ref):
    c = c_ref[...]
    sc = (c * (1.0 / (1.0 + jnp.exp(-c)))).astype(BF16)
    m = jnp.dot(sc, w_ref[0].astype(BF16), preferred_element_type=F32)
    o_ref[0] = m + b_ref[0]


def _adaln(c8, w4, b4):
    tn = 768
    n = 3 * D_MODEL
    return pl.pallas_call(
        _adaln_kernel,
        grid=(4, n // tn),
        in_specs=[pl.BlockSpec((8, D_MODEL), lambda s, j: (0, 0)),
                  pl.BlockSpec((1, D_MODEL, tn), lambda s, j: (s, 0, j)),
                  pl.BlockSpec((1, 1, tn), lambda s, j: (s, 0, j))],
        out_specs=pl.BlockSpec((1, 8, tn), lambda s, j: (s, 0, j)),
        out_shape=jax.ShapeDtypeStruct((4, 8, n), F32),
        compiler_params=_cparams(("arbitrary", "arbitrary")),
        name="adaln",
    )(c8, w4, b4)


def _rot_tab_kernel(pos_ref, invf_ref, o_ref):
    ang = pos_ref[...].astype(F32) * invf_ref[...]
    lane = lax.broadcasted_iota(jnp.int32, ang.shape, 1)
    c = jnp.where(lane < ROT_DIM, jnp.cos(ang), 1.0)
    s = jnp.sin(ang)
    s_lo = jnp.where(lane < ROT_HALF, -s, 0.0)
    s_hi = jnp.where((lane >= ROT_HALF) & (lane < ROT_DIM), s, 0.0)
    qs = HEAD_DIM ** -0.5
    o_ref[0] = c * qs
    o_ref[1] = s_lo * qs
    o_ref[2] = s_hi * qs
    o_ref[3] = c
    o_ref[4] = s_lo
    o_ref[5] = s_hi
    o_ref[6] = jnp.ones_like(c)
    o_ref[7] = jnp.zeros_like(c)
    o_ref[8] = jnp.zeros_like(c)


def _rot_tables(pos_col, invf):
    tm = 1024
    n = pos_col.shape[0]
    return pl.pallas_call(
        _rot_tab_kernel,
        grid=(n // tm,),
        in_specs=[pl.BlockSpec((tm, 1), lambda i: (i, 0)),
                  pl.BlockSpec((1, LANES), lambda i: (0, 0))],
        out_specs=pl.BlockSpec((9, tm, LANES), lambda i: (0, i, 0)),
        out_shape=jax.ShapeDtypeStruct((9, n, LANES), F32),
        compiler_params=_cparams(("arbitrary",)),
        name="rot_tables",
    )(pos_col, invf)


QKV_TM = 1024
QKV_TN = 512


def _rotate(t, c, s_lo, s_hi):
    return (t * c + pltpu.roll(t, LANES - ROT_HALF, axis=1) * s_lo
            + pltpu.roll(t, ROT_HALF, axis=1) * s_hi)


QKV_NI = TOKENS // QKV_TM
QKV_NJ = 3 * D_MODEL // QKV_TN
QKV_TILES = QKV_NI * QKV_NJ


def _qkv_kernel(x_ref, sc_ref, sh_ref, w_ref, tab_ref, o_ref, h_ref, acc_a, acc_b, *, dil):
    g = pl.program_id(0)

    @pl.when(g % QKV_NJ == 0)
    def _():
        h_ref[...] = (x_ref[...] * (1.0 + sc_ref[0]) + sh_ref[0]).astype(BF16)

    @pl.when(g == 0)
    def _():
        acc_b[...] = jnp.zeros_like(acc_b)

    rows = QKV_TM // dil

    def step(read_ref, write_ref):
        acc = jnp.dot(h_ref[...], w_ref[0].astype(BF16), preferred_element_type=F32)
        for hc in range(QKV_TN // LANES):
            write_ref[hc] = acc[:, hc * LANES:(hc + 1) * LANES]
        for hc in range(QKV_TN // LANES):
            sl = slice(hc * LANES, (hc + 1) * LANES)
            for r in range(dil):
                rs = slice(r * rows, (r + 1) * rows)
                t = read_ref[hc] if dil == 1 else read_ref[hc, pl.ds(r, rows, stride=dil), :]
                o_ref[0, r, :, sl] = _rotate(t, tab_ref[0, rs], tab_ref[1, rs], tab_ref[2, rs]).astype(BF16)

    @pl.when(g % 2 == 0)
    def _():
        step(acc_b, acc_a)

    @pl.when(g % 2 == 1)
    def _():
        step(acc_a, acc_b)


def _qkv(x2, scale, shift, w_qkv, tabs, grp, dil):
    tm, tn = QKV_TM, QKV_TN
    per_b = SEQ // tm

    def cur(g):
        t = jnp.minimum(g, QKV_TILES - 1)
        return t // QKV_NJ, t % QKV_NJ

    def prv(g):
        t = jnp.maximum(g - 1, 0)
        return t // QKV_NJ, t % QKV_NJ

    return pl.pallas_call(
        functools.partial(_qkv_kernel, dil=dil),
        grid=(QKV_TILES + 1,),
        in_specs=[pl.BlockSpec((tm, D_MODEL), lambda g: (cur(g)[0], 0)),
                  pl.BlockSpec((1, 1, D_MODEL), lambda g: (cur(g)[0] // per_b, 0, 0)),
                  pl.BlockSpec((1, 1, D_MODEL), lambda g: (cur(g)[0] // per_b, 0, 0)),
                  pl.BlockSpec((1, D_MODEL, tn), lambda g: (0, 0, grp * QKV_NJ + cur(g)[1])),
                  pl.BlockSpec((3, tm, LANES),
                               lambda g: (prv(g)[1] // (D_MODEL // tn), grp * QKV_NI + prv(g)[0], 0))],
        out_specs=pl.BlockSpec((1, dil, tm // dil, tn),
                               lambda g: (prv(g)[0] // per_b, 0, prv(g)[0] % per_b, prv(g)[1])),
        out_shape=jax.ShapeDtypeStruct((BATCH, dil, SEQ // dil, 3 * D_MODEL), BF16),
        scratch_shapes=[pltpu.VMEM((tm, D_MODEL), BF16),
                        pltpu.VMEM((tn // LANES, tm, LANES), F32),
                        pltpu.VMEM((tn // LANES, tm, LANES), F32)],
        compiler_params=_cparams(("arbitrary",)),
        name=f"qkv_proj_dil{dil}",
    )(x2, scale, shift, w_qkv, tabs)


ATT_TQ = 128
ATT_TK = ATT_TQ + 2 * RADIUS


def _attn_kernel(q_ref, kp_ref, kc_ref, kn_ref, vp_ref, vc_ref, vn_ref, o_ref, lse_ref, kbuf, vbuf, *, seq_len):
    i = pl.program_id(2)
    kbuf[0:RADIUS] = kp_ref[0, 0]
    kbuf[RADIUS:RADIUS + ATT_TQ] = kc_ref[0, 0]
    kbuf[RADIUS + ATT_TQ:ATT_TK] = kn_ref[0, 0]
    vbuf[0:RADIUS] = vp_ref[0, 0]
    vbuf[RADIUS:RADIUS + ATT_TQ] = vc_ref[0, 0]
    vbuf[RADIUS + ATT_TQ:ATT_TK] = vn_ref[0, 0]

    qpos = i * ATT_TQ + lax.broadcasted_iota(jnp.int32, (ATT_TQ, ATT_TK), 0)
    kpos = i * ATT_TQ - RADIUS + lax.broadcasted_iota(jnp.int32, (ATT_TQ, ATT_TK), 1)
    valid = (jnp.abs(qpos - kpos) <= RADIUS) & (kpos >= 0) & (kpos < seq_len)
    bias = jnp.where(valid, 0.0, NEG_INF).astype(F32)
    lane = lax.broadcasted_iota(jnp.int32, (ATT_TQ, LANES), 1)
    lse_all = jnp.zeros((ATT_TQ, LANES), F32)
    for h in range(HEADS):
        sl = slice(h * HEAD_DIM, (h + 1) * HEAD_DIM)
        s = lax.dot_general(q_ref[0, 0, :, sl], kbuf[:, sl], (((1,), (1,)), ((), ())),
                            preferred_element_type=F32) + bias
        m = jnp.max(s, axis=-1, keepdims=True)
        p = jnp.exp(s - m)
        den = jnp.sum(p, axis=-1, keepdims=True)
        o = jnp.dot(p.astype(BF16), vbuf[:, sl], preferred_element_type=F32) / den
        o_ref[0, 0, :, sl] = o.astype(BF16)
        lse_all = jnp.where(lane == h, m + jnp.log(den), lse_all)
    lse_ref[0, 0] = lse_all


def _attention_group(qkv, dil):
    seq_len = SEQ // dil
    tq = ATT_TQ
    sub = tq // RADIUS
    n_sub = seq_len // RADIUS

    def cur(which):
        return pl.BlockSpec((1, 1, tq, D_MODEL), lambda b, r, i: (b, r, i, which))

    def prev(which):
        return pl.BlockSpec((1, 1, RADIUS, D_MODEL), lambda b, r, i: (b, r, jnp.maximum(i * sub - 1, 0), which))

    def nxt(which):
        return pl.BlockSpec((1, 1, RADIUS, D_MODEL),
                            lambda b, r, i: (b, r, jnp.minimum((i + 1) * sub, n_sub - 1), which))

    return pl.pallas_call(
        functools.partial(_attn_kernel, seq_len=seq_len),
        grid=(BATCH, dil, seq_len // tq),
        in_specs=[cur(0), prev(1), cur(1), nxt(1), prev(2), cur(2), nxt(2)],
        out_specs=[pl.BlockSpec((1, 1, tq, D_MODEL), lambda b, r, i: (b, r, i, 0)),
                   pl.BlockSpec((1, 1, tq, LANES), lambda b, r, i: (b, r, i, 0))],
        out_shape=[jax.ShapeDtypeStruct((BATCH, dil, seq_len, D_MODEL), BF16),
                   jax.ShapeDtypeStruct((BATCH, dil, seq_len, LANES), F32)],
        scratch_shapes=[pltpu.VMEM((ATT_TK, D_MODEL), BF16), pltpu.VMEM((ATT_TK, D_MODEL), BF16)],
        compiler_params=_cparams(("arbitrary", "arbitrary", "arbitrary")),
        name=f"attn_dil{dil}",
    )(qkv, qkv, qkv, qkv, qkv, qkv, qkv)


def _deepnorm(x, y, gate, g, b):
    z = ALPHA * x + gate * y
    mu = jnp.mean(z, axis=-1, keepdims=True)
    zc = z - mu
    var = jnp.mean(zc * zc, axis=-1, keepdims=True)
    return zc * lax.rsqrt(var + LN_EPS) * g + b


def _split_bf16(a):
    hi = a.astype(BF16)
    lo = (a - hi.astype(F32)).astype(BF16)
    return hi, lo


def _router_logits(h, wr, br):
    h_hi, h_lo = _split_bf16(h)
    w_hi, w_lo = _split_bf16(wr)
    acc = jnp.dot(h_hi, w_hi, preferred_element_type=F32)
    acc = acc + jnp.dot(h_hi, w_lo, preferred_element_type=F32)
    acc = acc + jnp.dot(h_lo, w_hi, preferred_element_type=F32)
    return acc + br


def _store_slabs(ref, val):
    tm = val.shape[0]
    for s in range(SLABS):
        lo = val[:, (2 * s) * LANES:(2 * s + 1) * LANES].astype(BF16).astype(F32)
        hi = val[:, (2 * s + 1) * LANES:(2 * s + 2) * LANES].astype(BF16).astype(F32)
        word = ((lax.bitcast_convert_type(lo, U32) >> 16)
                | (lax.bitcast_convert_type(hi, U32) & jnp.uint32(0xFFFF0000)))
        ref[pl.ds(s, tm, stride=SLABS), :] = word


def _load_slabs(ref, base, tm):
    parts = []
    for s in range(SLABS):
        word = ref[pl.ds(base + s, tm, stride=SLABS), :]
        parts.append(lax.bitcast_convert_type(word << 16, F32))
        parts.append(lax.bitcast_convert_type(word & jnp.uint32(0xFFFF0000), F32))
    return jnp.concatenate(parts, axis=1)


PROJ_TM = 256


def _proj_kernel(*refs, merge):
    if merge:
        (o0, o1, o2, l0, l1, l2, w_ref, x_ref, gate_ref, g_ref, b_ref, sc_ref, sh_ref, wr_ref, br_ref,
         xn_ref, hs_ref, lg_ref, on1, on2, ln1, ln2) = refs
        for src, dst, dil in ((o1, on1, DILATIONS[1]), (o2, on2, DILATIONS[2])):
            for r in range(dil):
                for h in range(HEADS):
                    dst[h, pl.ds(r, PROJ_TM // dil, stride=dil), :] = (
                        src[0, r, :, h * HEAD_DIM:(h + 1) * HEAD_DIM].astype(F32))
        for src, dst, dil in ((l1, ln1, DILATIONS[1]), (l2, ln2, DILATIONS[2])):
            for r in range(dil):
                dst[pl.ds(r, PROJ_TM // dil, stride=dil), :] = src[0, r]
        ls = [l0[0, 0], ln1[...], ln2[...]]
        mx = jnp.maximum(jnp.maximum(ls[0], ls[1]), ls[2])
        es = [jnp.exp(l - mx) for l in ls]
        inv = 1.0 / (es[0] + es[1] + es[2])
        ws = [e * inv for e in es]
        parts = []
        for h in range(HEADS):
            sl = slice(h * HEAD_DIM, (h + 1) * HEAD_DIM)
            acc = ws[0][:, h:h + 1] * o0[0, 0, :, sl].astype(F32)
            acc = acc + ws[1][:, h:h + 1] * on1[h]
            acc = acc + ws[2][:, h:h + 1] * on2[h]
            parts.append(acc.astype(BF16))
        y_in = jnp.concatenate(parts, axis=1)
    else:
        (y_ref, w_ref, x_ref, gate_ref, g_ref, b_ref, sc_ref, sh_ref, wr_ref, br_ref,
         xn_ref, hs_ref, lg_ref) = refs
        y_in = y_ref[...]
    y = jnp.dot(y_in, w_ref[...], preferred_element_type=F32)
    xn = _deepnorm(x_ref[...], y, gate_ref[0], g_ref[...], b_ref[...])
    xn_ref[...] = xn
    h = xn * (1.0 + sc_ref[0]) + sh_ref[0]
    _store_slabs(hs_ref, h)
    lg_ref[...] = _router_logits(h, wr_ref[...], br_ref[...])


def _proj_ln_router(mix_in, w_bf, x2, gate, ln_g, ln_b, scale, shift, wr, br, *, merge):
    tm = PROJ_TM
    per_b = SEQ // tm
    row = lambda i: (i, 0)
    full = lambda i: (0, 0)
    per_batch = lambda i: (i // per_b, 0, 0)
    scratch = []
    if merge:
        def resid(width, dil):
            return pl.BlockSpec((1, dil, tm // dil, width), lambda i: (i // per_b, 0, i % per_b, 0))
        mix_specs = ([resid(D_MODEL, dil) for dil in DILATIONS] + [resid(LANES, dil) for dil in DILATIONS])
        scratch = [pltpu.VMEM((HEADS, tm, HEAD_DIM), F32), pltpu.VMEM((HEADS, tm, HEAD_DIM), F32),
                   pltpu.VMEM((tm, LANES), F32), pltpu.VMEM((tm, LANES), F32)]
    else:
        mix_specs = [pl.BlockSpec((tm, D_MODEL), row)]
    return pl.pallas_call(
        functools.partial(_proj_kernel, merge=merge),
        grid=(TOKENS // tm,),
        in_specs=mix_specs + [
            pl.BlockSpec((D_MODEL, D_MODEL), full),
            pl.BlockSpec((tm, D_MODEL), row),
            pl.BlockSpec((1, 1, D_MODEL), per_batch),
            pl.BlockSpec((1, D_MODEL), full),
            pl.BlockSpec((1, D_MODEL), full),
            pl.BlockSpec((1, 1, D_MODEL), per_batch),
            pl.BlockSpec((1, 1, D_MODEL), per_batch),
            pl.BlockSpec((D_MODEL, LANES), full),
            pl.BlockSpec((1, LANES), full)],
        out_specs=[pl.BlockSpec((tm, D_MODEL), row),
                   pl.BlockSpec((tm * SLABS, LANES), row),
                   pl.BlockSpec((tm, LANES), row)],
        out_shape=[jax.ShapeDtypeStruct((TOKENS, D_MODEL), F32),
                   jax.ShapeDtypeStruct((TOKENS * SLABS, LANES), U32),
                   jax.ShapeDtypeStruct((TOKENS, LANES), F32)],
        scratch_shapes=scratch,
        compiler_params=_cparams(("arbitrary",)),
        name="proj_merge" if merge else "proj",
    )(*mix_in, w_bf, x2, gate, ln_g, ln_b, scale, shift, wr, br)


def _slab_rows(row):
    if isinstance(row, int):
        return pl.ds(row * SLABS, SLABS)
    return pl.ds(pl.multiple_of(row * SLABS, SLABS), SLABS)


def _row_copy(src, dst, s_row, d_row, sem):
    return pltpu.make_async_copy(src.at[_slab_rows(s_row)], dst.at[_slab_rows(d_row)], sem)


MOE_SLOTS = 3


def _moe_kernel(be_ref, nv_ref, rt_ref, hs_hbm, wg_ref, wu_ref, wd_ref, ys_ref, xbuf, wg_bf, wu_bf, wd_bf, sems):
    s = pl.program_id(0)
    blk = s - 1

    def gather(b):
        slot = b % MOE_SLOTS

        def issue(k, carry):
            _row_copy(hs_hbm, xbuf, rt_ref[b * MOE_BLK + k], slot * MOE_BLK + k, sems.at[slot]).start()
            return carry
        lax.fori_loop(0, MOE_BLK, issue, 0, unroll=8)

    def used(b):
        return (b < MOE_NB) & (nv_ref[jnp.minimum(b, MOE_NB - 1)] > 0)

    @pl.when(s == 0)
    def _():
        for b in range(MOE_SLOTS - 1):
            @pl.when(used(b))
            def _():
                gather(b)

    ahead = blk + MOE_SLOTS - 1

    @pl.when((s > 0) & used(ahead))
    def _():
        gather(ahead)

    nv = nv_ref[jnp.maximum(blk, 0)]

    @pl.when((s > 0) & (nv > 0))
    def _():
        slot = blk % MOE_SLOTS

        def drain(k, carry):
            _row_copy(hs_hbm, xbuf, 0, 0, sems.at[slot]).wait()
            return carry
        lax.fori_loop(0, MOE_BLK, drain, 0, unroll=8)

        x = _load_slabs(xbuf, pl.multiple_of(slot * (MOE_BLK * SLABS), MOE_BLK * SLABS), MOE_BLK)
        rows = lax.broadcasted_iota(jnp.int32, (MOE_BLK, 1), 0)
        x = jnp.where(rows < nv, x, 0.0).astype(BF16)
        g = jnp.dot(x, wg_bf[...], preferred_element_type=F32)
        u = jnp.dot(x, wu_bf[...], preferred_element_type=F32)
        a = (g * (1.0 / (1.0 + jnp.exp(-g)))) * u
        y = jnp.dot(a.astype(BF16), wd_bf[...], preferred_element_type=F32)
        _store_slabs(ys_ref, y)

    @pl.when((s > 0) & (nv <= 0))
    def _():
        ys_ref[...] = jnp.zeros_like(ys_ref)

    new_expert = (s == 0) | (be_ref[jnp.minimum(s, MOE_NB - 1)] != be_ref[jnp.maximum(blk, 0)])

    @pl.when(used(s) & new_expert)
    def _():
        wg_bf[...] = wg_ref[0, 0].astype(BF16)
        wu_bf[...] = wu_ref[0, 0].astype(BF16)
        wd_bf[...] = wd_ref[0, 0].astype(BF16)


def _moe_ffn(block_e, block_nv, row_tok, hs, w_gate, w_up, w_down, layer):
    wmap = lambda s, be, nv, rt: (layer, be[jnp.minimum(s, MOE_NB - 1)], 0, 0)
    return pl.pallas_call(
        _moe_kernel,
        grid_spec=pltpu.PrefetchScalarGridSpec(
            num_scalar_prefetch=3,
            grid=(MOE_NB + 1,),
            in_specs=[pl.BlockSpec(memory_space=pl.ANY),
                      pl.BlockSpec((1, 1, D_MODEL, EXPERT_DIM), wmap),
                      pl.BlockSpec((1, 1, D_MODEL, EXPERT_DIM), wmap),
                      pl.BlockSpec((1, 1, EXPERT_DIM, D_MODEL), wmap)],
            out_specs=pl.BlockSpec((MOE_BLK * SLABS, LANES), lambda s, be, nv, rt: (jnp.maximum(s - 1, 0), 0)),
            scratch_shapes=[pltpu.VMEM((MOE_SLOTS * MOE_BLK * SLABS, LANES), U32),
                            pltpu.VMEM((D_MODEL, EXPERT_DIM), BF16),
                            pltpu.VMEM((D_MODEL, EXPERT_DIM), BF16),
                            pltpu.VMEM((EXPERT_DIM, D_MODEL), BF16),
                            pltpu.SemaphoreType.DMA((MOE_SLOTS,))]),
        out_shape=jax.ShapeDtypeStruct((MOE_ROWS * SLABS, LANES), U32),
        compiler_params=pltpu.CompilerParams(dimension_semantics=("arbitrary",), vmem_limit_bytes=VMEM_LIMIT,
                                             disable_bounds_checks=True),
        name="moe_ffn",
    )(block_e, block_nv, row_tok, hs, w_gate, w_up, w_down)


COMB_TM = 256


def _combine_kernel(dest_ref, ys_hbm, gt_ref, x_ref, gate_ref, g_ref, b_ref, *rest, modulate):
    if modulate:
        sc_ref, sh_ref, xn_ref, hn_ref, buf, sem = rest
    else:
        xn_ref, buf, sem = rest
    i = pl.program_id(0)
    tm = COMB_TM

    def copy(src_row, slot):
        return _row_copy(ys_hbm, buf, src_row, slot, sem)

    def issue(t, carry):
        tok = i * tm + t
        copy(dest_ref[2 * tok], t).start(priority=0)
        copy(dest_ref[2 * tok + 1], tm + t).start(priority=1)
        return carry

    lax.fori_loop(0, tm, issue, 0)

    def drain(t, carry):
        copy(0, 0).wait()
        copy(0, 0).wait()
        return carry

    lax.fori_loop(0, tm, drain, 0)

    gt = gt_ref[...]
    y = _load_slabs(buf, 0, tm) * gt[:, 0:1] + _load_slabs(buf, tm * SLABS, tm) * gt[:, 1:2]
    xn = _deepnorm(x_ref[...], y, gate_ref[0], g_ref[...], b_ref[...])
    xn_ref[...] = xn
    if modulate:
        hn_ref[...] = (xn * (1.0 + sc_ref[0]) + sh_ref[0]).astype(BF16)


def _combine(dest, ys, gates, x2, gate, ln_g, ln_b, scale=None, shift=None):
    tm = COMB_TM
    per_b = SEQ // tm
    modulate = scale is not None
    row = lambda i, d: (i, 0)
    full = lambda i, d: (0, 0)
    per_batch = lambda i, d: (i // per_b, 0, 0)
    in_specs = [pl.BlockSpec(memory_space=pl.ANY),
                pl.BlockSpec((tm, 2), row),
                pl.BlockSpec((tm, D_MODEL), row),
                pl.BlockSpec((1, 1, D_MODEL), per_batch),
                pl.BlockSpec((1, D_MODEL), full),
                pl.BlockSpec((1, D_MODEL), full)]
    out_specs = [pl.BlockSpec((tm, D_MODEL), row)]
    out_shape = [jax.ShapeDtypeStruct((TOKENS, D_MODEL), F32)]
    args = [dest, ys, gates, x2, gate, ln_g, ln_b]
    if modulate:
        in_specs += [pl.BlockSpec((1, 1, D_MODEL), per_batch)] * 2
        out_specs.append(pl.BlockSpec((tm, D_MODEL), row))
        out_shape.append(jax.ShapeDtypeStruct((TOKENS, D_MODEL), BF16))
        args += [scale, shift]
    return pl.pallas_call(
        functools.partial(_combine_kernel, modulate=modulate),
        grid_spec=pltpu.PrefetchScalarGridSpec(
            num_scalar_prefetch=1,
            grid=(TOKENS // tm,),
            in_specs=in_specs,
            out_specs=out_specs,
            scratch_shapes=[pltpu.VMEM((2 * tm * SLABS, LANES), U32), pltpu.SemaphoreType.DMA(())]),
        out_shape=out_shape,
        compiler_params=_cparams(("arbitrary",)),
        name="moe_combine",
    )(*args)


def _route(logits):
    lc = logits[:, :N_GROUPS]
    lf = logits[:, N_GROUPS:N_GROUPS + N_EXPERTS].reshape(TOKENS, N_GROUPS, EPG)
    pc = jax.nn.softmax(lc, axis=-1)
    pg = jnp.max(pc, axis=-1, keepdims=True)
    gi = jnp.argmax(pc, axis=-1)[:, None]
    lf_sel = jnp.take_along_axis(lf, gi[:, :, None], axis=1)[:, 0]
    pf = jax.nn.softmax(lf_sel, axis=-1)
    e1 = jnp.argmax(pf, axis=-1)[:, None]
    p1 = jnp.max(pf, axis=-1, keepdims=True)
    pf2 = jnp.where(jnp.arange(EPG)[None, :] == e1, -1.0, pf)
    e2 = jnp.argmax(pf2, axis=-1)[:, None]
    p2 = jnp.max(pf2, axis=-1, keepdims=True)
    pk = jnp.concatenate([p1, p2], axis=1)
    ek = jnp.concatenate([e1, e2], axis=1)
    gates = pg * (pk / jnp.sum(pk, axis=-1, keepdims=True))
    e_flat = (gi * EPG + ek).reshape(-1).astype(jnp.int32)
    onehot = (e_flat[:, None] == jnp.arange(N_EXPERTS, dtype=jnp.int32)[None, :]).astype(jnp.int32)
    csum = jnp.cumsum(onehot, axis=0)
    rank = jnp.sum((csum - onehot) * onehot, axis=1)
    counts = csum[-1]
    padded = (counts + MOE_BLK - 1) // MOE_BLK * MOE_BLK
    pends = jnp.cumsum(padded)
    pstarts = pends - padded
    dest = (pstarts[e_flat] + rank).astype(jnp.int32)
    blk_start = jnp.arange(MOE_NB, dtype=jnp.int32) * MOE_BLK
    block_e = jnp.clip(jnp.searchsorted(pends, blk_start, side='right'), 0, N_EXPERTS - 1).astype(jnp.int32)
    block_nv = jnp.clip(counts[block_e] - (blk_start - pstarts[block_e]), 0, MOE_BLK)
    block_nv = jnp.where(blk_start < pends[-1], block_nv, 0).astype(jnp.int32)
    row_tok = jnp.zeros((MOE_ROWS,), jnp.int32).at[dest].set(jnp.arange(2 * TOKENS, dtype=jnp.int32) // 2)
    return gates, dest, row_tok, block_e, block_nv


def _moe(hs, logits, x2, gate, ln_g, ln_b, w_gate, w_up, w_down, layer, scale=None, shift=None):
    gates, dest, row_tok, block_e, block_nv = _route(logits)
    ys = _moe_ffn(block_e, block_nv, row_tok, hs, w_gate, w_up, w_down, layer)
    return _combine(dest, ys, gates, x2, gate, ln_g, ln_b, scale, shift)


FN_TM = 256


def _fnet_in_kernel(h_ref, w_ref, cs_ref, ab_ref):
    u = jnp.dot(h_ref[...], w_ref[...], preferred_element_type=F32).astype(BF16)
    for g in range(FNET_GROUPS):
        sl = slice(g * FNET_CH, (g + 1) * FNET_CH)
        ab = jnp.dot(u[:, sl], cs_ref[...], preferred_element_type=F32)
        ab_ref[0, 0, :, sl] = ab[:, :FNET_CH].astype(BF16)
        ab_ref[0, 1, :, sl] = ab[:, FNET_CH:].astype(BF16)


def _fnet_in(h_bf, w_in_bf, cs):
    tm = FN_TM
    per_b = SEQ // tm
    return pl.pallas_call(
        _fnet_in_kernel,
        grid=(TOKENS // tm,),
        in_specs=[pl.BlockSpec((tm, D_MODEL), lambda i: (i, 0)),
                  pl.BlockSpec((D_MODEL, D_MODEL), lambda i: (0, 0)),
                  pl.BlockSpec((FNET_CH, 2 * FNET_CH), lambda i: (0, 0))],
        out_specs=pl.BlockSpec((1, 2, tm, D_MODEL), lambda i: (i // per_b, 0, i % per_b, 0)),
        out_shape=jax.ShapeDtypeStruct((BATCH, 2, SEQ, D_MODEL), BF16),
        compiler_params=_cparams(("arbitrary",)),
        name="fnet_in",
    )(h_bf, w_in_bf, cs)


DFT_R = 4
DFT_Q = SEQ // DFT_R
DFT_TN = 256
DFT_CHUNK = 256
_C4 = (1, 0, -1, 0)
_S4 = (0, 1, 0, -1)


def _signed_sum(terms):
    acc = None
    for sign, v in terms:
        if sign == 0:
            continue
        if acc is None:
            acc = v if sign > 0 else -v
        else:
            acc = acc + v if sign > 0 else acc - v
    return acc


def _dft_kernel(a_ref, b_ref, m_hbm, y_ref, m_scr, ab_scr, y_scr, sem):
    @pl.when((pl.program_id(0) == 0) & (pl.program_id(1) == 0))
    def _():
        cp = pltpu.make_async_copy(m_hbm, m_scr, sem)
        cp.start()
        cp.wait()

    for ch in range(DFT_Q // DFT_CHUNK):
        rows = [slice(m * DFT_Q + ch * DFT_CHUNK, m * DFT_Q + (ch + 1) * DFT_CHUNK) for m in range(DFT_R)]
        a = [a_ref[0, 0, r, :].astype(F32) for r in rows]
        b = [b_ref[0, 0, r, :].astype(F32) for r in rows]
        for rho in range(DFT_R):
            q = [(rho * m) % 4 for m in range(DFT_R)]
            ap = _signed_sum([(_C4[q[m]], a[m]) for m in range(DFT_R)] + [(-_S4[q[m]], b[m]) for m in range(DFT_R)])
            bp = _signed_sum([(_S4[q[m]], a[m]) for m in range(DFT_R)] + [(_C4[q[m]], b[m]) for m in range(DFT_R)])
            ab_scr[rho, ch * DFT_CHUNK:(ch + 1) * DFT_CHUNK, :] = ap.astype(BF16)
            ab_scr[rho, DFT_Q + ch * DFT_CHUNK:DFT_Q + (ch + 1) * DFT_CHUNK, :] = bp.astype(BF16)

    for rho in range(DFT_R):
        y = jnp.dot(m_scr[rho], ab_scr[rho], preferred_element_type=F32)
        for c in range(DFT_TN // LANES):
            y_scr[c, pl.ds(rho, DFT_Q, stride=DFT_R), :] = y[:, c * LANES:(c + 1) * LANES]
    for c in range(DFT_TN // LANES):
        y_ref[0, :, c * LANES:(c + 1) * LANES] = y_scr[c].astype(BF16)


def _seq_dft(mats, ab):
    tn = DFT_TN
    return pl.pallas_call(
        _dft_kernel,
        grid=(BATCH, D_MODEL // tn),
        in_specs=[pl.BlockSpec((1, 1, SEQ, tn), lambda b, j: (b, 0, 0, j)),
                  pl.BlockSpec((1, 1, SEQ, tn), lambda b, j: (b, 1, 0, j)),
                  pl.BlockSpec(memory_space=pl.ANY)],
        out_specs=pl.BlockSpec((1, SEQ, tn), lambda b, j: (b, 0, j)),
        out_shape=jax.ShapeDtypeStruct((BATCH, SEQ, D_MODEL), BF16),
        scratch_shapes=[pltpu.VMEM((DFT_R, DFT_Q, 2 * DFT_Q), BF16),
                        pltpu.VMEM((DFT_R, 2 * DFT_Q, tn), BF16),
                        pltpu.VMEM((tn // LANES, SEQ, LANES), F32),
                        pltpu.SemaphoreType.DMA(())],
        compiler_params=_cparams(("arbitrary", "arbitrary")),
        name="seq_dft",
    )(ab, ab, mats)


def _dft_constants():
    n = np.arange(FNET_CH)
    ang = 2.0 * np.pi * ((n[:, None] * n[None, :]) % FNET_CH) / FNET_CH
    cs = np.concatenate([np.cos(ang), np.sin(ang)], axis=1) / math.sqrt(FNET_CH)
    return jnp.asarray(cs, dtype=BF16)


def _seq_dft_matrices():
    shape = (DFT_R, DFT_Q, DFT_Q)
    rho = lax.broadcasted_iota(jnp.int32, shape, 0)
    kp = lax.broadcasted_iota(jnp.int32, shape, 1)
    jp = lax.broadcasted_iota(jnp.int32, shape, 2)
    ang = (((DFT_R * kp + rho) * jp) & (SEQ - 1)).astype(F32) * (2.0 * math.pi / SEQ)
    norm = 1.0 / math.sqrt(SEQ)
    return jnp.concatenate([(jnp.cos(ang) * norm).astype(BF16), (jnp.sin(ang) * (-norm)).astype(BF16)], axis=2)


def kernel(x, c, positions, ada_w, ada_b, attn_w_qkv, attn_w_o, fnet_w_in, fnet_w_out, ln_g, ln_b,
           router_coarse_w, router_coarse_b, router_fine_w, router_fine_b,
           expert_w_gate, expert_w_up, expert_w_down):
    x2 = x.reshape(TOKENS, D_MODEL)

    c8 = jnp.zeros((8, D_MODEL), F32).at[:BATCH].set(c)
    mod = _adaln(c8, ada_w.reshape(4, D_MODEL, 3 * D_MODEL), ada_b.reshape(4, 1, 3 * D_MODEL))[:, :BATCH]

    def mod_rows(idx):
        m = mod[idx].reshape(BATCH, 3, 1, D_MODEL)
        return m[:, 0], m[:, 1], m[:, 2]

    def router_params(i):
        wr = jnp.zeros((D_MODEL, LANES), F32)
        wr = wr.at[:, :N_GROUPS].set(router_coarse_w[i]).at[:, N_GROUPS:N_GROUPS + N_EXPERTS].set(router_fine_w[i])
        br = jnp.zeros((1, LANES), F32)
        br = br.at[0, :N_GROUPS].set(router_coarse_b[i]).at[0, N_GROUPS:N_GROUPS + N_EXPERTS].set(router_fine_b[i])
        return wr, br

    inv_freq = np.float32(ROPE_THETA) ** (-np.arange(0, ROT_DIM, 2, dtype=np.float32) / np.float32(ROT_DIM))
    invf = np.zeros((1, LANES), np.float32)
    invf[0, :ROT_HALF] = inv_freq
    invf[0, ROT_HALF:ROT_DIM] = inv_freq
    pos_rm = [positions.reshape(BATCH, SEQ // QKV_TM, QKV_TM // dil, dil).transpose(0, 1, 3, 2).reshape(TOKENS)
              for dil in DILATIONS]
    tabs = _rot_tables(jnp.concatenate(pos_rm).reshape(N_DIL * TOKENS, 1), jnp.asarray(invf))

    shift, scale, gate = mod_rows(0)
    outs, lses = [], []
    for g, dil in enumerate(DILATIONS):
        o_g, lse_g = _attention_group(_qkv(x2, scale, shift, attn_w_qkv, tabs, g, dil), dil)
        outs.append(o_g)
        lses.append(lse_g)
    shift2, scale2, gate2 = mod_rows(1)
    wr, br = router_params(0)
    x2, hs, logits = _proj_ln_router(outs + lses, attn_w_o[0].astype(BF16), x2, gate, ln_g[0, 0][None], ln_b[0, 0][None],
                                     scale2, shift2, wr, br, merge=True)
    shift3, scale3, gate3 = mod_rows(2)
    x2, h_bf = _moe(hs, logits, x2, gate2, ln_g[0, 1][None], ln_b[0, 1][None],
                    expert_w_gate, expert_w_up, expert_w_down, 0, scale3, shift3)

    ab = _fnet_in(h_bf, fnet_w_in[0].astype(BF16), _dft_constants())
    y = _seq_dft(_seq_dft_matrices(), ab).reshape(TOKENS, D_MODEL)
    shift4, scale4, gate4 = mod_rows(3)
    wr, br = router_params(1)
    x2, hs, logits = _proj_ln_router([y], fnet_w_out[0].astype(BF16), x2, gate3, ln_g[1, 0][None], ln_b[1, 0][None],
                                     scale4, shift4, wr, br, merge=False)
    (x2,) = _moe(hs, logits, x2, gate4, ln_g[1, 1][None], ln_b[1, 1][None],
                 expert_w_gate, expert_w_up, expert_w_down, 1)
    return x2.reshape(BATCH, SEQ, D_MODEL)
```

```python
import functools
import math

import numpy as np
import jax
import jax.numpy as jnp
from jax import lax
from jax.experimental import pallas as pl
from jax.experimental.pallas import tpu as pltpu

F32 = jnp.float32
BF16 = jnp.bfloat16
U32 = jnp.uint32

D_MODEL = 2048
BATCH = 2
SEQ = 4096
TOKENS = BATCH * SEQ
DEPTH = 2
HEAD_DIM = 128
HEADS = D_MODEL // HEAD_DIM
DILATIONS = (1, 4, 16)
RADIUS = 64
N_DIL = 3
QKV_COLS = N_DIL * 3 * D_MODEL
ROT_DIM = HEAD_DIM // 4
ROT_HALF = ROT_DIM // 2
ROPE_THETA = 500000.0
FNET_GROUPS = 4
FNET_CH = D_MODEL // FNET_GROUPS
N_GROUPS = 4
EPG = 8
N_EXPERTS = N_GROUPS * EPG
EXPERT_DIM = D_MODEL // 4
LN_EPS = 1e-5
NEG_INF = -1e30
ALPHA = (2 * DEPTH) ** 0.25

LANES = 128
SLABS = D_MODEL // (2 * LANES)
VMEM_LIMIT = 56 * 1024 * 1024

MOE_BLK = 256
MOE_NB = (2 * TOKENS) // MOE_BLK + N_EXPERTS
MOE_ROWS = MOE_NB * MOE_BLK


def _cparams(sem):
    return pltpu.CompilerParams(dimension_semantics=sem, vmem_limit_bytes=VMEM_LIMIT)


def _adaln_kernel(c_ref, w_ref, b_ref, o_ref):
    c = c_ref[...]
    sc = (c * (1.0 / (1.0 + jnp.exp(-c)))).astype(BF16)
    m = jnp.dot(sc, w_ref[0].astype(BF16), preferred_element_type=F32)
    o_ref[0] = m + b_ref[0]


def _adaln(c8, w4, b4):
    tn = 768
    n = 3 * D_MODEL
    return pl.pallas_call(
        _adaln_kernel,
        grid=(4, n // tn),
        in_specs=[pl.BlockSpec((8, D_MODEL), lambda s, j: (0, 0)),
                  pl.BlockSpec((1, D_MODEL, tn), lambda s, j: (s, 0, j)),
                  pl.BlockSpec((1, 1, tn), lambda s, j: (s, 0, j))],
        out_specs=pl.BlockSpec((1, 8, tn), lambda s, j: (s, 0, j)),
        out_shape=jax.ShapeDtypeStruct((4, 8, n), F32),
        compiler_params=_cparams(("arbitrary", "arbitrary")),
        name="adaln",
    )(c8, w4, b4)


def _rot_tab_kernel(pos_ref, invf_ref, o_ref):
    ang = pos_ref[...].astype(F32) * invf_ref[...]
    lane = lax.broadcasted_iota(jnp.int32, ang.shape, 1)
    c = jnp.where(lane < ROT_DIM, jnp.cos(ang), 1.0)
    s = jnp.sin(ang)
    s_lo = jnp.where(lane < ROT_HALF, -s, 0.0)
    s_hi = jnp.where((lane >= ROT_HALF) & (lane < ROT_DIM), s, 0.0)
    qs = HEAD_DIM ** -0.5
    o_ref[0] = c * qs
    o_ref[1] = s_lo * qs
    o_ref[2] = s_hi * qs
    o_ref[3] = c
    o_ref[4] = s_lo
    o_ref[5] = s_hi
    o_ref[6] = jnp.ones_like(c)
    o_ref[7] = jnp.zeros_like(c)
    o_ref[8] = jnp.zeros_like(c)


def _rot_tables(pos_col, invf):
    tm = 1024
    n = pos_col.shape[0]
    return pl.pallas_call(
        _rot_tab_kernel,
        grid=(n // tm,),
        in_specs=[pl.BlockSpec((tm, 1), lambda i: (i, 0)),
                  pl.BlockSpec((1, LANES), lambda i: (0, 0))],
        out_specs=pl.BlockSpec((9, tm, LANES), lambda i: (0, i, 0)),
        out_shape=jax.ShapeDtypeStruct((9, n, LANES), F32),
        compiler_params=_cparams(("arbitrary",)),
        name="rot_tables",
    )(pos_col, invf)


QKV_TM = 1024
QKV_TN = 512


def _rotate(t, c, s_lo, s_hi):
    return (t * c + pltpu.roll(t, LANES - ROT_HALF, axis=1) * s_lo
            + pltpu.roll(t, ROT_HALF, axis=1) * s_hi)


QKV_NI = TOKENS // QKV_TM
QKV_NJ = 3 * D_MODEL // QKV_TN
QKV_TILES = QKV_NI * QKV_NJ


def _qkv_kernel(x_ref, sc_ref, sh_ref, w_ref, tab_ref, o_ref, h_ref, acc_a, acc_b, *, dil):
    g = pl.program_id(0)

    @pl.when(g % QKV_NJ == 0)
    def _():
        h_ref[...] = (x_ref[...] * (1.0 + sc_ref[0]) + sh_ref[0]).astype(BF16)

    @pl.when(g == 0)
    def _():
        acc_b[...] = jnp.zeros_like(acc_b)

    rows = QKV_TM // dil

    def step(read_ref, write_ref):
        acc = jnp.dot(h_ref[...], w_ref[0].astype(BF16), preferred_element_type=F32)
        for hc in range(QKV_TN // LANES):
            write_ref[hc] = acc[:, hc * LANES:(hc + 1) * LANES]
        for hc in range(QKV_TN // LANES):
            sl = slice(hc * LANES, (hc + 1) * LANES)
            for r in range(dil):
                rs = slice(r * rows, (r + 1) * rows)
                t = read_ref[hc] if dil == 1 else read_ref[hc, pl.ds(r, rows, stride=dil), :]
                o_ref[0, r, :, sl] = _rotate(t, tab_ref[0, rs], tab_ref[1, rs], tab_ref[2, rs]).astype(BF16)

    @pl.when(g % 2 == 0)
    def _():
        step(acc_b, acc_a)

    @pl.when(g % 2 == 1)
    def _():
        step(acc_a, acc_b)


def _qkv(x2, scale, shift, w_qkv, tabs, grp, dil):
    tm, tn = QKV_TM, QKV_TN
    per_b = SEQ // tm

    def cur(g):
        t = jnp.minimum(g, QKV_TILES - 1)
        return t // QKV_NJ, t % QKV_NJ

    def prv(g):
        t = jnp.maximum(g - 1, 0)
        return t // QKV_NJ, t % QKV_NJ

    return pl.pallas_call(
        functools.partial(_qkv_kernel, dil=dil),
        grid=(QKV_TILES + 1,),
        in_specs=[pl.BlockSpec((tm, D_MODEL), lambda g: (cur(g)[0], 0)),
                  pl.BlockSpec((1, 1, D_MODEL), lambda g: (cur(g)[0] // per_b, 0, 0)),
                  pl.BlockSpec((1, 1, D_MODEL), lambda g: (cur(g)[0] // per_b, 0, 0)),
                  pl.BlockSpec((1, D_MODEL, tn), lambda g: (0, 0, grp * QKV_NJ + cur(g)[1])),
                  pl.BlockSpec((3, tm, LANES),
                               lambda g: (prv(g)[1] // (D_MODEL // tn), grp * QKV_NI + prv(g)[0], 0))],
        out_specs=pl.BlockSpec((1, dil, tm // dil, tn),
                               lambda g: (prv(g)[0] // per_b, 0, prv(g)[0] % per_b, prv(g)[1])),
        out_shape=jax.ShapeDtypeStruct((BATCH, dil, SEQ // dil, 3 * D_MODEL), BF16),
        scratch_shapes=[pltpu.VMEM((tm, D_MODEL), BF16),
                        pltpu.VMEM((tn // LANES, tm, LANES), F32),
                        pltpu.VMEM((tn // LANES, tm, LANES), F32)],
        compiler_params=_cparams(("arbitrary",)),
        name=f"qkv_proj_dil{dil}",
    )(x2, scale, shift, w_qkv, tabs)


ATT_TQ = 128
ATT_TK = ATT_TQ + 2 * RADIUS


def _attn_kernel(q_ref, kp_ref, kc_ref, kn_ref, vp_ref, vc_ref, vn_ref, o_ref, lse_ref, kbuf, vbuf, *, seq_len):
    i = pl.program_id(2)
    kbuf[0:RADIUS] = kp_ref[0, 0]
    kbuf[RADIUS:RADIUS + ATT_TQ] = kc_ref[0, 0]
    kbuf[RADIUS + ATT_TQ:ATT_TK] = kn_ref[0, 0]
    vbuf[0:RADIUS] = vp_ref[0, 0]
    vbuf[RADIUS:RADIUS + ATT_TQ] = vc_ref[0, 0]
    vbuf[RADIUS + ATT_TQ:ATT_TK] = vn_ref[0, 0]

    qpos = i * ATT_TQ + lax.broadcasted_iota(jnp.int32, (ATT_TQ, ATT_TK), 0)
    kpos = i * ATT_TQ - RADIUS + lax.broadcasted_iota(jnp.int32, (ATT_TQ, ATT_TK), 1)
    valid = (jnp.abs(qpos - kpos) <= RADIUS) & (kpos >= 0) & (kpos < seq_len)
    bias = jnp.where(valid, 0.0, NEG_INF).astype(F32)
    lane = lax.broadcasted_iota(jnp.int32, (ATT_TQ, LANES), 1)
    lse_all = jnp.zeros((ATT_TQ, LANES), F32)
    for h in range(HEADS):
        sl = slice(h * HEAD_DIM, (h + 1) * HEAD_DIM)
        s = lax.dot_general(q_ref[0, 0, :, sl], kbuf[:, sl], (((1,), (1,)), ((), ())),
                            preferred_element_type=F32) + bias
        m = jnp.max(s, axis=-1, keepdims=True)
        p = jnp.exp(s - m)
        den = jnp.sum(p, axis=-1, keepdims=True)
        o = jnp.dot(p.astype(BF16), vbuf[:, sl], preferred_element_type=F32) / den
        o_ref[0, 0, :, sl] = o.astype(BF16)
        lse_all = jnp.where(lane == h, m + jnp.log(den), lse_all)
    lse_ref[0, 0] = lse_all


def _attention_group(qkv, dil):
    seq_len = SEQ // dil
    tq = ATT_TQ
    sub = tq // RADIUS
    n_sub = seq_len // RADIUS

    def cur(which):
        return pl.BlockSpec((1, 1, tq, D_MODEL), lambda b, r, i: (b, r, i, which))

    def prev(which):
        return pl.BlockSpec((1, 1, RADIUS, D_MODEL), lambda b, r, i: (b, r, jnp.maximum(i * sub - 1, 0), which))

    def nxt(which):
        return pl.BlockSpec((1, 1, RADIUS, D_MODEL),
                            lambda b, r, i: (b, r, jnp.minimum((i + 1) * sub, n_sub - 1), which))

    return pl.pallas_call(
        functools.partial(_attn_kernel, seq_len=seq_len),
        grid=(BATCH, dil, seq_len // tq),
        in_specs=[cur(0), prev(1), cur(1), nxt(1), prev(2), cur(2), nxt(2)],
        out_specs=[pl.BlockSpec((1, 1, tq, D_MODEL), lambda b, r, i: (b, r, i, 0)),
                   pl.BlockSpec((1, 1, tq, LANES), lambda b, r, i: (b, r, i, 0))],
        out_shape=[jax.ShapeDtypeStruct((BATCH, dil, seq_len, D_MODEL), BF16),
                   jax.ShapeDtypeStruct((BATCH, dil, seq_len, LANES), F32)],
        scratch_shapes=[pltpu.VMEM((ATT_TK, D_MODEL), BF16), pltpu.VMEM((ATT_TK, D_MODEL), BF16)],
        compiler_params=_cparams(("arbitrary", "arbitrary", "arbitrary")),
        name=f"attn_dil{dil}",
    )(qkv, qkv, qkv, qkv, qkv, qkv, qkv)


def _deepnorm(x, y, gate, g, b):
    z = ALPHA * x + gate * y
    mu = jnp.mean(z, axis=-1, keepdims=True)
    zc = z - mu
    var = jnp.mean(zc * zc, axis=-1, keepdims=True)
    return zc * lax.rsqrt(var + LN_EPS) * g + b


def _split_bf16(a):
    hi = a.astype(BF16)
    lo = (a - hi.astype(F32)).astype(BF16)
    return hi, lo


def _router_logits(h, wr, br):
    h_hi, h_lo = _split_bf16(h)
    w_hi, w_lo = _split_bf16(wr)
    acc = jnp.dot(h_hi, w_hi, preferred_element_type=F32)
    acc = acc + jnp.dot(h_hi, w_lo, preferred_element_type=F32)
    acc = acc + jnp.dot(h_lo, w_hi, preferred_element_type=F32)
    return acc + br


def _store_slabs(ref, val):
    tm = val.shape[0]
    for s in range(SLABS):
        lo = val[:, (2 * s) * LANES:(2 * s + 1) * LANES].astype(BF16).astype(F32)
        hi = val[:, (2 * s + 1) * LANES:(2 * s + 2) * LANES].astype(BF16).astype(F32)
        word = ((lax.bitcast_convert_type(lo, U32) >> 16)
                | (lax.bitcast_convert_type(hi, U32) & jnp.uint32(0xFFFF0000)))
        ref[pl.ds(s, tm, stride=SLABS), :] = word


def _load_slabs(ref, base, tm):
    parts = []
    for s in range(SLABS):
        word = ref[pl.ds(base + s, tm, stride=SLABS), :]
        parts.append(lax.bitcast_convert_type(word << 16, F32))
        parts.append(lax.bitcast_convert_type(word & jnp.uint32(0xFFFF0000), F32))
    return jnp.concatenate(parts, axis=1)


PROJ_TM = 256


def _proj_kernel(*refs, merge):
    if merge:
        (o0, o1, o2, l0, l1, l2, w_ref, x_ref, gate_ref, g_ref, b_ref, sc_ref, sh_ref, wr_ref, br_ref,
         xn_ref, hs_ref, lg_ref, on1, on2, ln1, ln2) = refs
        for src, dst, dil in ((o1, on1, DILATIONS[1]), (o2, on2, DILATIONS[2])):
            for r in range(dil):
                for h in range(HEADS):
                    dst[h, pl.ds(r, PROJ_TM // dil, stride=dil), :] = (
                        src[0, r, :, h * HEAD_DIM:(h + 1) * HEAD_DIM].astype(F32))
        for src, dst, dil in ((l1, ln1, DILATIONS[1]), (l2, ln2, DILATIONS[2])):
            for r in range(dil):
                dst[pl.ds(r, PROJ_TM // dil, stride=dil), :] = src[0, r]
        ls = [l0[0, 0], ln1[...], ln2[...]]
        mx = jnp.maximum(jnp.maximum(ls[0], ls[1]), ls[2])
        es = [jnp.exp(l - mx) for l in ls]
        inv = 1.0 / (es[0] + es[1] + es[2])
        ws = [e * inv for e in es]
        parts = []
        for h in range(HEADS):
            sl = slice(h * HEAD_DIM, (h + 1) * HEAD_DIM)
            acc = ws[0][:, h:h + 1] * o0[0, 0, :, sl].astype(F32)
            acc = acc + ws[1][:, h:h + 1] * on1[h]
            acc = acc + ws[2][:, h:h + 1] * on2[h]
            parts.append(acc.astype(BF16))
        y_in = jnp.concatenate(parts, axis=1)
    else:
        (y_ref, w_ref, x_ref, gate_ref, g_ref, b_ref, sc_ref, sh_ref, wr_ref, br_ref,
         xn_ref, hs_ref, lg_ref) = refs
        y_in = y_ref[...]
    y = jnp.dot(y_in, w_ref[...], preferred_element_type=F32)
    xn = _deepnorm(x_ref[...], y, gate_ref[0], g_ref[...], b_ref[...])
    xn_ref[...] = xn
    h = xn * (1.0 + sc_ref[0]) + sh_ref[0]
    _store_slabs(hs_ref, h)
    lg_ref[...] = _router_logits(h, wr_ref[...], br_ref[...])


def _proj_ln_router(mix_in, w_bf, x2, gate, ln_g, ln_b, scale, shift, wr, br, *, merge):
    tm = PROJ_TM
    per_b = SEQ // tm
    row = lambda i: (i, 0)
    full = lambda i: (0, 0)
    per_batch = lambda i: (i // per_b, 0, 0)
    scratch = []
    if merge:
        def resid(width, dil):
            return pl.BlockSpec((1, dil, tm // dil, width), lambda i: (i // per_b, 0, i % per_b, 0))
        mix_specs = ([resid(D_MODEL, dil) for dil in DILATIONS] + [resid(LANES, dil) for dil in DILATIONS])
        scratch = [pltpu.VMEM((HEADS, tm, HEAD_DIM), F32), pltpu.VMEM((HEADS, tm, HEAD_DIM), F32),
                   pltpu.VMEM((tm, LANES), F32), pltpu.VMEM((tm, LANES), F32)]
    else:
        mix_specs = [pl.BlockSpec((tm, D_MODEL), row)]
    return pl.pallas_call(
        functools.partial(_proj_kernel, merge=merge),
        grid=(TOKENS // tm,),
        in_specs=mix_specs + [
            pl.BlockSpec((D_MODEL, D_MODEL), full),
            pl.BlockSpec((tm, D_MODEL), row),
            pl.BlockSpec((1, 1, D_MODEL), per_batch),
            pl.BlockSpec((1, D_MODEL), full),
            pl.BlockSpec((1, D_MODEL), full),
            pl.BlockSpec((1, 1, D_MODEL), per_batch),
            pl.BlockSpec((1, 1, D_MODEL), per_batch),
            pl.BlockSpec((D_MODEL, LANES), full),
            pl.BlockSpec((1, LANES), full)],
        out_specs=[pl.BlockSpec((tm, D_MODEL), row),
                   pl.BlockSpec((tm * SLABS, LANES), row),
                   pl.BlockSpec((tm, LANES), row)],
        out_shape=[jax.ShapeDtypeStruct((TOKENS, D_MODEL), F32),
                   jax.ShapeDtypeStruct((TOKENS * SLABS, LANES), U32),
                   jax.ShapeDtypeStruct((TOKENS, LANES), F32)],
        scratch_shapes=scratch,
        compiler_params=_cparams(("arbitrary",)),
        name="proj_merge" if merge else "proj",
    )(*mix_in, w_bf, x2, gate, ln_g, ln_b, scale, shift, wr, br)


def _slab_rows(row):
    if isinstance(row, int):
        return pl.ds(row * SLABS, SLABS)
    return pl.ds(pl.multiple_of(row * SLABS, SLABS), SLABS)


def _row_copy(src, dst, s_row, d_row, sem):
    return pltpu.make_async_copy(src.at[_slab_rows(s_row)], dst.at[_slab_rows(d_row)], sem)


MOE_SLOTS = 3


def _moe_kernel(be_ref, nv_ref, rt_ref, hs_hbm, wg_ref, wu_ref, wd_ref, ys_ref, xbuf, wg_bf, wu_bf, wd_bf, sems):
    s = pl.program_id(0)
    blk = s - 1

    def gather(b):
        slot = b % MOE_SLOTS

        def issue(k, carry):
            _row_copy(hs_hbm, xbuf, rt_ref[b * MOE_BLK + k], slot * MOE_BLK + k, sems.at[slot]).start()
            return carry
        lax.fori_loop(0, MOE_BLK, issue, 0, unroll=8)

    def used(b):
        return (b < MOE_NB) & (nv_ref[jnp.minimum(b, MOE_NB - 1)] > 0)

    @pl.when(s == 0)
    def _():
        for b in range(MOE_SLOTS - 1):
            @pl.when(used(b))
            def _():
                gather(b)

    ahead = blk + MOE_SLOTS - 1

    @pl.when((s > 0) & used(ahead))
    def _():
        gather(ahead)

    nv = nv_ref[jnp.maximum(blk, 0)]

    @pl.when((s > 0) & (nv > 0))
    def _():
        slot = blk % MOE_SLOTS

        def drain(k, carry):
            _row_copy(hs_hbm, xbuf, 0, 0, sems.at[slot]).wait()
            return carry
        lax.fori_loop(0, MOE_BLK, drain, 0, unroll=8)

        x = _load_slabs(xbuf, pl.multiple_of(slot * (MOE_BLK * SLABS), MOE_BLK * SLABS), MOE_BLK)
        rows = lax.broadcasted_iota(jnp.int32, (MOE_BLK, 1), 0)
        x = jnp.where(rows < nv, x, 0.0).astype(BF16)
        g = jnp.dot(x, wg_bf[...], preferred_element_type=F32)
        u = jnp.dot(x, wu_bf[...], preferred_element_type=F32)
        a = (g * (1.0 / (1.0 + jnp.exp(-g)))) * u
        y = jnp.dot(a.astype(BF16), wd_bf[...], preferred_element_type=F32)
        _store_slabs(ys_ref, y)

    @pl.when((s > 0) & (nv <= 0))
    def _():
        ys_ref[...] = jnp.zeros_like(ys_ref)

    new_expert = (s == 0) | (be_ref[jnp.minimum(s, MOE_NB - 1)] != be_ref[jnp.maximum(blk, 0)])

    @pl.when(used(s) & new_expert)
    def _():
        wg_bf[...] = wg_ref[0, 0].astype(BF16)
        wu_bf[...] = wu_ref[0, 0].astype(BF16)
        wd_bf[...] = wd_ref[0, 0].astype(BF16)


def _moe_ffn(block_e, block_nv, row_tok, hs, w_gate, w_up, w_down, layer):
    wmap = lambda s, be, nv, rt: (layer, be[jnp.minimum(s, MOE_NB - 1)], 0, 0)
    return pl.pallas_call(
        _moe_kernel,
        grid_spec=pltpu.PrefetchScalarGridSpec(
            num_scalar_prefetch=3,
            grid=(MOE_NB + 1,),
            in_specs=[pl.BlockSpec(memory_space=pl.ANY),
                      pl.BlockSpec((1, 1, D_MODEL, EXPERT_DIM), wmap),
                      pl.BlockSpec((1, 1, D_MODEL, EXPERT_DIM), wmap),
                      pl.BlockSpec((1, 1, EXPERT_DIM, D_MODEL), wmap)],
            out_specs=pl.BlockSpec((MOE_BLK * SLABS, LANES), lambda s, be, nv, rt: (jnp.maximum(s - 1, 0), 0)),
            scratch_shapes=[pltpu.VMEM((MOE_SLOTS * MOE_BLK * SLABS, LANES), U32),
                            pltpu.VMEM((D_MODEL, EXPERT_DIM), BF16),
                            pltpu.VMEM((D_MODEL, EXPERT_DIM), BF16),
                            pltpu.VMEM((EXPERT_DIM, D_MODEL), BF16),
                            pltpu.SemaphoreType.DMA((MOE_SLOTS,))]),
        out_shape=jax.ShapeDtypeStruct((MOE_ROWS * SLABS, LANES), U32),
        compiler_params=pltpu.CompilerParams(dimension_semantics=("arbitrary",), vmem_limit_bytes=VMEM_LIMIT,
                                             disable_bounds_checks=True),
        name="moe_ffn",
    )(block_e, block_nv, row_tok, hs, w_gate, w_up, w_down)


COMB_TM = 256


def _combine_kernel(dest_ref, ys_hbm, gt_ref, x_ref, gate_ref, g_ref, b_ref, *rest, modulate):
    if modulate:
        sc_ref, sh_ref, xn_ref, hn_ref, buf, sems = rest
    else:
        xn_ref, buf, sems = rest
    i = pl.program_id(0)
    tm = COMB_TM

    def gather(tile, slot):
        def issue(t, carry):
            tok = tile * tm + t
            for k in range(2):
                _row_copy(ys_hbm, buf, dest_ref[2 * tok + k], (2 * slot + k) * tm + t, sems.at[slot]).start()
            return carry
        lax.fori_loop(0, tm, issue, 0, unroll=4)

    @pl.when(i == 0)
    def _():
        gather(0, 0)

    @pl.when(i + 1 < pl.num_programs(0))
    def _():
        gather(i + 1, (i + 1) % 2)

    slot = i % 2

    def drain(t, carry):
        _row_copy(ys_hbm, buf, 0, 0, sems.at[slot]).wait()
        _row_copy(ys_hbm, buf, 0, 0, sems.at[slot]).wait()
        return carry

    lax.fori_loop(0, tm, drain, 0, unroll=4)

    gt = gt_ref[...]
    base = pl.multiple_of(slot * (2 * tm * SLABS), 2 * tm * SLABS)
    y = _load_slabs(buf, base, tm) * gt[:, 0:1] + _load_slabs(buf, base + tm * SLABS, tm) * gt[:, 1:2]
    xn = _deepnorm(x_ref[...], y, gate_ref[0], g_ref[...], b_ref[...])
    xn_ref[...] = xn
    if modulate:
        hn_ref[...] = (xn * (1.0 + sc_ref[0]) + sh_ref[0]).astype(BF16)


def _combine(dest, ys, gates, x2, gate, ln_g, ln_b, scale=None, shift=None):
    tm = COMB_TM
    per_b = SEQ // tm
    modulate = scale is not None
    row = lambda i, d: (i, 0)
    full = lambda i, d: (0, 0)
    per_batch = lambda i, d: (i // per_b, 0, 0)
    in_specs = [pl.BlockSpec(memory_space=pl.ANY),
                pl.BlockSpec((tm, 2), row),
                pl.BlockSpec((tm, D_MODEL), row),
                pl.BlockSpec((1, 1, D_MODEL), per_batch),
                pl.BlockSpec((1, D_MODEL), full),
                pl.BlockSpec((1, D_MODEL), full)]
    out_specs = [pl.BlockSpec((tm, D_MODEL), row)]
    out_shape = [jax.ShapeDtypeStruct((TOKENS, D_MODEL), F32)]
    args = [dest, ys, gates, x2, gate, ln_g, ln_b]
    if modulate:
        in_specs += [pl.BlockSpec((1, 1, D_MODEL), per_batch)] * 2
        out_specs.append(pl.BlockSpec((tm, D_MODEL), row))
        out_shape.append(jax.ShapeDtypeStruct((TOKENS, D_MODEL), BF16))
        args += [scale, shift]
    return pl.pallas_call(
        functools.partial(_combine_kernel, modulate=modulate),
        grid_spec=pltpu.PrefetchScalarGridSpec(
            num_scalar_prefetch=1,
            grid=(TOKENS // tm,),
            in_specs=in_specs,
            out_specs=out_specs,
            scratch_shapes=[pltpu.VMEM((4 * tm * SLABS, LANES), U32), pltpu.SemaphoreType.DMA((2,))]),
        out_shape=out_shape,
        compiler_params=pltpu.CompilerParams(dimension_semantics=("arbitrary",), vmem_limit_bytes=VMEM_LIMIT,
                                             disable_bounds_checks=True),
        name="moe_combine",
    )(*args)


def _route(logits):
    lc = logits[:, :N_GROUPS]
    lf = logits[:, N_GROUPS:N_GROUPS + N_EXPERTS].reshape(TOKENS, N_GROUPS, EPG)
    pc = jax.nn.softmax(lc, axis=-1)
    pg = jnp.max(pc, axis=-1, keepdims=True)
    gi = jnp.argmax(pc, axis=-1)[:, None]
    lf_sel = jnp.take_along_axis(lf, gi[:, :, None], axis=1)[:, 0]
    pf = jax.nn.softmax(lf_sel, axis=-1)
    e1 = jnp.argmax(pf, axis=-1)[:, None]
    p1 = jnp.max(pf, axis=-1, keepdims=True)
    pf2 = jnp.where(jnp.arange(EPG)[None, :] == e1, -1.0, pf)
    e2 = jnp.argmax(pf2, axis=-1)[:, None]
    p2 = jnp.max(pf2, axis=-1, keepdims=True)
    pk = jnp.concatenate([p1, p2], axis=1)
    ek = jnp.concatenate([e1, e2], axis=1)
    gates = pg * (pk / jnp.sum(pk, axis=-1, keepdims=True))
    e_flat = (gi * EPG + ek).reshape(-1).astype(jnp.int32)
    onehot = (e_flat[:, None] == jnp.arange(N_EXPERTS, dtype=jnp.int32)[None, :]).astype(jnp.int32)
    csum = jnp.cumsum(onehot, axis=0)
    rank = jnp.sum((csum - onehot) * onehot, axis=1)
    counts = csum[-1]
    padded = (counts + MOE_BLK - 1) // MOE_BLK * MOE_BLK
    pends = jnp.cumsum(padded)
    pstarts = pends - padded
    dest = (pstarts[e_flat] + rank).astype(jnp.int32)
    blk_start = jnp.arange(MOE_NB, dtype=jnp.int32) * MOE_BLK
    block_e = jnp.clip(jnp.searchsorted(pends, blk_start, side='right'), 0, N_EXPERTS - 1).astype(jnp.int32)
    block_nv = jnp.clip(counts[block_e] - (blk_start - pstarts[block_e]), 0, MOE_BLK)
    block_nv = jnp.where(blk_start < pends[-1], block_nv, 0).astype(jnp.int32)
    row_tok = jnp.zeros((MOE_ROWS,), jnp.int32).at[dest].set(jnp.arange(2 * TOKENS, dtype=jnp.int32) // 2)
    return gates, dest, row_tok, block_e, block_nv


def _moe(hs, logits, x2, gate, ln_g, ln_b, w_gate, w_up, w_down, layer, scale=None, shift=None):
    gates, dest, row_tok, block_e, block_nv = _route(logits)
    ys = _moe_ffn(block_e, block_nv, row_tok, hs, w_gate, w_up, w_down, layer)
    return _combine(dest, ys, gates, x2, gate, ln_g, ln_b, scale, shift)


FN_TM = 256


def _fnet_in_kernel(h_ref, w_ref, cs_ref, ab_ref):
    u = jnp.dot(h_ref[...], w_ref[...], preferred_element_type=F32).astype(BF16)
    for g in range(FNET_GROUPS):
        sl = slice(g * FNET_CH, (g + 1) * FNET_CH)
        ab = jnp.dot(u[:, sl], cs_ref[...], preferred_element_type=F32)
        ab_ref[0, 0, :, sl] = ab[:, :FNET_CH].astype(BF16)
        ab_ref[0, 1, :, sl] = ab[:, FNET_CH:].astype(BF16)


def _fnet_in(h_bf, w_in_bf, cs):
    tm = FN_TM
    per_b = SEQ // tm
    return pl.pallas_call(
        _fnet_in_kernel,
        grid=(TOKENS // tm,),
        in_specs=[pl.BlockSpec((tm, D_MODEL), lambda i: (i, 0)),
                  pl.BlockSpec((D_MODEL, D_MODEL), lambda i: (0, 0)),
                  pl.BlockSpec((FNET_CH, 2 * FNET_CH), lambda i: (0, 0))],
        out_specs=pl.BlockSpec((1, 2, tm, D_MODEL), lambda i: (i // per_b, 0, i % per_b, 0)),
        out_shape=jax.ShapeDtypeStruct((BATCH, 2, SEQ, D_MODEL), BF16),
        compiler_params=_cparams(("arbitrary",)),
        name="fnet_in",
    )(h_bf, w_in_bf, cs)


DFT_R = 4
DFT_Q = SEQ // DFT_R
DFT_TN = 256
DFT_CHUNK = 256
_C4 = (1, 0, -1, 0)
_S4 = (0, 1, 0, -1)


def _signed_sum(terms):
    acc = None
    for sign, v in terms:
        if sign == 0:
            continue
        if acc is None:
            acc = v if sign > 0 else -v
        else:
            acc = acc + v if sign > 0 else acc - v
    return acc


def _dft_kernel(a_ref, b_ref, m_hbm, y_ref, m_scr, ab_scr, y_scr, sem):
    @pl.when((pl.program_id(0) == 0) & (pl.program_id(1) == 0))
    def _():
        cp = pltpu.make_async_copy(m_hbm, m_scr, sem)
        cp.start()
        cp.wait()

    for ch in range(DFT_Q // DFT_CHUNK):
        rows = [slice(m * DFT_Q + ch * DFT_CHUNK, m * DFT_Q + (ch + 1) * DFT_CHUNK) for m in range(DFT_R)]
        a = [a_ref[0, 0, r, :].astype(F32) for r in rows]
        b = [b_ref[0, 0, r, :].astype(F32) for r in rows]
        for rho in range(DFT_R):
            q = [(rho * m) % 4 for m in range(DFT_R)]
            ap = _signed_sum([(_C4[q[m]], a[m]) for m in range(DFT_R)] + [(-_S4[q[m]], b[m]) for m in range(DFT_R)])
            bp = _signed_sum([(_S4[q[m]], a[m]) for m in range(DFT_R)] + [(_C4[q[m]], b[m]) for m in range(DFT_R)])
            ab_scr[rho, ch * DFT_CHUNK:(ch + 1) * DFT_CHUNK, :] = ap.astype(BF16)
            ab_scr[rho, DFT_Q + ch * DFT_CHUNK:DFT_Q + (ch + 1) * DFT_CHUNK, :] = bp.astype(BF16)

    for rho in range(DFT_R):
        y = jnp.dot(m_scr[rho], ab_scr[rho], preferred_element_type=F32)
        for c in range(DFT_TN // LANES):
            y_scr[c, pl.ds(rho, DFT_Q, stride=DFT_R), :] = y[:, c * LANES:(c + 1) * LANES]
    for c in range(DFT_TN // LANES):
        y_ref[0, :, c * LANES:(c + 1) * LANES] = y_scr[c].astype(BF16)


def _seq_dft(mats, ab):
    tn = DFT_TN
    return pl.pallas_call(
        _dft_kernel,
        grid=(BATCH, D_MODEL // tn),
        in_specs=[pl.BlockSpec((1, 1, SEQ, tn), lambda b, j: (b, 0, 0, j)),
                  pl.BlockSpec((1, 1, SEQ, tn), lambda b, j: (b, 1, 0, j)),
                  pl.BlockSpec(memory_space=pl.ANY)],
        out_specs=pl.BlockSpec((1, SEQ, tn), lambda b, j: (b, 0, j)),
        out_shape=jax.ShapeDtypeStruct((BATCH, SEQ, D_MODEL), BF16),
        scratch_shapes=[pltpu.VMEM((DFT_R, DFT_Q, 2 * DFT_Q), BF16),
                        pltpu.VMEM((DFT_R, 2 * DFT_Q, tn), BF16),
                        pltpu.VMEM((tn // LANES, SEQ, LANES), F32),
                        pltpu.SemaphoreType.DMA(())],
        compiler_params=_cparams(("arbitrary", "arbitrary")),
        name="seq_dft",
    )(ab, ab, mats)


def _dft_constants():
    n = np.arange(FNET_CH)
    ang = 2.0 * np.pi * ((n[:, None] * n[None, :]) % FNET_CH) / FNET_CH
    cs = np.concatenate([np.cos(ang), np.sin(ang)], axis=1) / math.sqrt(FNET_CH)
    return jnp.asarray(cs, dtype=BF16)


def _seq_dft_matrices():
    shape = (DFT_R, DFT_Q, DFT_Q)
    rho = lax.broadcasted_iota(jnp.int32, shape, 0)
    kp = lax.broadcasted_iota(jnp.int32, shape, 1)
    jp = lax.broadcasted_iota(jnp.int32, shape, 2)
    ang = (((DFT_R * kp + rho) * jp) & (SEQ - 1)).astype(F32) * (2.0 * math.pi / SEQ)
    norm = 1.0 / math.sqrt(SEQ)
    return jnp.concatenate([(jnp.cos(ang) * norm).astype(BF16), (jnp.sin(ang) * (-norm)).astype(BF16)], axis=2)


def kernel(x, c, positions, ada_w, ada_b, attn_w_qkv, attn_w_o, fnet_w_in, fnet_w_out, ln_g, ln_b,
           router_coarse_w, router_coarse_b, router_fine_w, router_fine_b,
           expert_w_gate, expert_w_up, expert_w_down):
    x2 = x.reshape(TOKENS, D_MODEL)

    c8 = jnp.zeros((8, D_MODEL), F32).at[:BATCH].set(c)
    mod = _adaln(c8, ada_w.reshape(4, D_MODEL, 3 * D_MODEL), ada_b.reshape(4, 1, 3 * D_MODEL))[:, :BATCH]

    def mod_rows(idx):
        m = mod[idx].reshape(BATCH, 3, 1, D_MODEL)
        return m[:, 0], m[:, 1], m[:, 2]

    def router_params(i):
        wr = jnp.zeros((D_MODEL, LANES), F32)
        wr = wr.at[:, :N_GROUPS].set(router_coarse_w[i]).at[:, N_GROUPS:N_GROUPS + N_EXPERTS].set(router_fine_w[i])
        br = jnp.zeros((1, LANES), F32)
        br = br.at[0, :N_GROUPS].set(router_coarse_b[i]).at[0, N_GROUPS:N_GROUPS + N_EXPERTS].set(router_fine_b[i])
        return wr, br

    inv_freq = np.float32(ROPE_THETA) ** (-np.arange(0, ROT_DIM, 2, dtype=np.float32) / np.float32(ROT_DIM))
    invf = np.zeros((1, LANES), np.float32)
    invf[0, :ROT_HALF] = inv_freq
    invf[0, ROT_HALF:ROT_DIM] = inv_freq
    pos_rm = [positions.reshape(BATCH, SEQ // QKV_TM, QKV_TM // dil, dil).transpose(0, 1, 3, 2).reshape(TOKENS)
              for dil in DILATIONS]
    tabs = _rot_tables(jnp.concatenate(pos_rm).reshape(N_DIL * TOKENS, 1), jnp.asarray(invf))

    shift, scale, gate = mod_rows(0)
    outs, lses = [], []
    for g, dil in enumerate(DILATIONS):
        o_g, lse_g = _attention_group(_qkv(x2, scale, shift, attn_w_qkv, tabs, g, dil), dil)
        outs.append(o_g)
        lses.append(lse_g)
    shift2, scale2, gate2 = mod_rows(1)
    wr, br = router_params(0)
    x2, hs, logits = _proj_ln_router(outs + lses, attn_w_o[0].astype(BF16), x2, gate, ln_g[0, 0][None], ln_b[0, 0][None],
                                     scale2, shift2, wr, br, merge=True)
    shift3, scale3, gate3 = mod_rows(2)
    x2, h_bf = _moe(hs, logits, x2, gate2, ln_g[0, 1][None], ln_b[0, 1][None],
                    expert_w_gate, expert_w_up, expert_w_down, 0, scale3, shift3)

    ab = _fnet_in(h_bf, fnet_w_in[0].astype(BF16), _dft_constants())
    y = _seq_dft(_seq_dft_matrices(), ab).reshape(TOKENS, D_MODEL)
    shift4, scale4, gate4 = mod_rows(3)
    wr, br = router_params(1)
    x2, hs, logits = _proj_ln_router([y], fnet_w_out[0].astype(BF16), x2, gate3, ln_g[1, 0][None], ln_b[1, 0][None],
                                     scale4, shift4, wr, br, merge=False)
    (x2,) = _moe(hs, logits, x2, gate4, ln_g[1, 1][None], ln_b[1, 1][None],
                 expert_w_gate, expert_w_up, expert_w_down, 1)
    return x2.reshape(BATCH, SEQ, D_MODEL)
```

```python
import functools
import math

import numpy as np
import jax
import jax.numpy as jnp
from jax import lax
from jax.experimental import pallas as pl
from jax.experimental.pallas import tpu as pltpu

F32 = jnp.float32
BF16 = jnp.bfloat16
U32 = jnp.uint32

D_MODEL = 2048
BATCH = 2
SEQ = 4096
TOKENS = BATCH * SEQ
DEPTH = 2
HEAD_DIM = 128
HEADS = D_MODEL // HEAD_DIM
DILATIONS = (1, 4, 16)
RADIUS = 64
N_DIL = 3
QKV_COLS = N_DIL * 3 * D_MODEL
ROT_DIM = HEAD_DIM // 4
ROT_HALF = ROT_DIM // 2
ROPE_THETA = 500000.0
FNET_GROUPS = 4
FNET_CH = D_MODEL // FNET_GROUPS
N_GROUPS = 4
EPG = 8
N_EXPERTS = N_GROUPS * EPG
EXPERT_DIM = D_MODEL // 4
LN_EPS = 1e-5
NEG_INF = -1e30
ALPHA = (2 * DEPTH) ** 0.25

LANES = 128
SLABS = D_MODEL // (2 * LANES)
VMEM_LIMIT = 56 * 1024 * 1024

MOE_BLK = 256
MOE_NB = (2 * TOKENS) // MOE_BLK + N_EXPERTS
MOE_ROWS = MOE_NB * MOE_BLK


def _cparams(sem):
    return pltpu.CompilerParams(dimension_semantics=sem, vmem_limit_bytes=VMEM_LIMIT)


def _adaln_kernel(c_ref, w_ref, b_ref, o_ref):
    c = c_ref[...]
    sc = (c * (1.0 / (1.0 + jnp.exp(-c)))).astype(BF16)
    m = jnp.dot(sc, w_ref[0].astype(BF16), preferred_element_type=F32)
    o_ref[0] = m + b_ref[0]


def _adaln(c8, w4, b4):
    tn = 768
    n = 3 * D_MODEL
    return pl.pallas_call(
        _adaln_kernel,
        grid=(4, n // tn),
        in_specs=[pl.BlockSpec((8, D_MODEL), lambda s, j: (0, 0)),
                  pl.BlockSpec((1, D_MODEL, tn), lambda s, j: (s, 0, j)),
                  pl.BlockSpec((1, 1, tn), lambda s, j: (s, 0, j))],
        out_specs=pl.BlockSpec((1, 8, tn), lambda s, j: (s, 0, j)),
        out_shape=jax.ShapeDtypeStruct((4, 8, n), F32),
        compiler_params=_cparams(("arbitrary", "arbitrary")),
        name="adaln",
    )(c8, w4, b4)


def _rot_tab_kernel(pos_ref, invf_ref, o_ref):
    ang = pos_ref[...].astype(F32) * invf_ref[...]
    lane = lax.broadcasted_iota(jnp.int32, ang.shape, 1)
    c = jnp.where(lane < ROT_DIM, jnp.cos(ang), 1.0)
    s = jnp.sin(ang)
    s_lo = jnp.where(lane < ROT_HALF, -s, 0.0)
    s_hi = jnp.where((lane >= ROT_HALF) & (lane < ROT_DIM), s, 0.0)
    qs = HEAD_DIM ** -0.5
    o_ref[0] = c * qs
    o_ref[1] = s_lo * qs
    o_ref[2] = s_hi * qs
    o_ref[3] = c
    o_ref[4] = s_lo
    o_ref[5] = s_hi
    o_ref[6] = jnp.ones_like(c)
    o_ref[7] = jnp.zeros_like(c)
    o_ref[8] = jnp.zeros_like(c)


def _rot_tables(pos_col, invf):
    tm = 1024
    n = pos_col.shape[0]
    return pl.pallas_call(
        _rot_tab_kernel,
        grid=(n // tm,),
        in_specs=[pl.BlockSpec((tm, 1), lambda i: (i, 0)),
                  pl.BlockSpec((1, LANES), lambda i: (0, 0))],
        out_specs=pl.BlockSpec((9, tm, LANES), lambda i: (0, i, 0)),
        out_shape=jax.ShapeDtypeStruct((9, n, LANES), F32),
        compiler_params=_cparams(("arbitrary",)),
        name="rot_tables",
    )(pos_col, invf)


QKV_TM = 1024
QKV_TN = 512


def _rotate(t, c, s_lo, s_hi):
    return (t * c + pltpu.roll(t, LANES - ROT_HALF, axis=1) * s_lo
            + pltpu.roll(t, ROT_HALF, axis=1) * s_hi)


QKV_NI = TOKENS // QKV_TM
QKV_NJ = 3 * D_MODEL // QKV_TN
QKV_TILES = QKV_NI * QKV_NJ


def _qkv_kernel(x_ref, sc_ref, sh_ref, w_ref, tab_ref, o_ref, h_ref, acc_a, acc_b, *, dil):
    g = pl.program_id(0)

    @pl.when(g % QKV_NJ == 0)
    def _():
        h_ref[...] = (x_ref[...] * (1.0 + sc_ref[0]) + sh_ref[0]).astype(BF16)

    @pl.when(g == 0)
    def _():
        acc_b[...] = jnp.zeros_like(acc_b)

    rows = QKV_TM // dil

    def step(read_ref, write_ref):
        acc = jnp.dot(h_ref[...], w_ref[0].astype(BF16), preferred_element_type=F32)
        for hc in range(QKV_TN // LANES):
            write_ref[hc] = acc[:, hc * LANES:(hc + 1) * LANES]
        for hc in range(QKV_TN // LANES):
            sl = slice(hc * LANES, (hc + 1) * LANES)
            for r in range(dil):
                rs = slice(r * rows, (r + 1) * rows)
                t = read_ref[hc] if dil == 1 else read_ref[hc, pl.ds(r, rows, stride=dil), :]
                o_ref[0, r, :, sl] = _rotate(t, tab_ref[0, rs], tab_ref[1, rs], tab_ref[2, rs]).astype(BF16)

    @pl.when(g % 2 == 0)
    def _():
        step(acc_b, acc_a)

    @pl.when(g % 2 == 1)
    def _():
        step(acc_a, acc_b)


def _qkv(x2, scale, shift, w_qkv, tabs, grp, dil):
    tm, tn = QKV_TM, QKV_TN
    per_b = SEQ // tm

    def cur(g):
        t = jnp.minimum(g, QKV_TILES - 1)
        return t // QKV_NJ, t % QKV_NJ

    def prv(g):
        t = jnp.maximum(g - 1, 0)
        return t // QKV_NJ, t % QKV_NJ

    return pl.pallas_call(
        functools.partial(_qkv_kernel, dil=dil),
        grid=(QKV_TILES + 1,),
        in_specs=[pl.BlockSpec((tm, D_MODEL), lambda g: (cur(g)[0], 0)),
                  pl.BlockSpec((1, 1, D_MODEL), lambda g: (cur(g)[0] // per_b, 0, 0)),
                  pl.BlockSpec((1, 1, D_MODEL), lambda g: (cur(g)[0] // per_b, 0, 0)),
                  pl.BlockSpec((1, D_MODEL, tn), lambda g: (0, 0, grp * QKV_NJ + cur(g)[1])),
                  pl.BlockSpec((3, tm, LANES),
                               lambda g: (prv(g)[1] // (D_MODEL // tn), grp * QKV_NI + prv(g)[0], 0))],
        out_specs=pl.BlockSpec((1, dil, tm // dil, tn),
                               lambda g: (prv(g)[0] // per_b, 0, prv(g)[0] % per_b, prv(g)[1])),
        out_shape=jax.ShapeDtypeStruct((BATCH, dil, SEQ // dil, 3 * D_MODEL), BF16),
        scratch_shapes=[pltpu.VMEM((tm, D_MODEL), BF16),
                        pltpu.VMEM((tn // LANES, tm, LANES), F32),
                        pltpu.VMEM((tn // LANES, tm, LANES), F32)],
        compiler_params=_cparams(("arbitrary",)),
        name=f"qkv_proj_dil{dil}",
    )(x2, scale, shift, w_qkv, tabs)


ATT_TQ = 128
ATT_TK = ATT_TQ + 2 * RADIUS


def _attn_kernel(q_ref, kp_ref, kc_ref, kn_ref, vp_ref, vc_ref, vn_ref, o_ref, lse_ref, kbuf, vbuf, *, seq_len):
    i = pl.program_id(2)
    kbuf[0:RADIUS] = kp_ref[0, 0]
    kbuf[RADIUS:RADIUS + ATT_TQ] = kc_ref[0, 0]
    kbuf[RADIUS + ATT_TQ:ATT_TK] = kn_ref[0, 0]
    vbuf[0:RADIUS] = vp_ref[0, 0]
    vbuf[RADIUS:RADIUS + ATT_TQ] = vc_ref[0, 0]
    vbuf[RADIUS + ATT_TQ:ATT_TK] = vn_ref[0, 0]

    qpos = i * ATT_TQ + lax.broadcasted_iota(jnp.int32, (ATT_TQ, ATT_TK), 0)
    kpos = i * ATT_TQ - RADIUS + lax.broadcasted_iota(jnp.int32, (ATT_TQ, ATT_TK), 1)
    valid = (jnp.abs(qpos - kpos) <= RADIUS) & (kpos >= 0) & (kpos < seq_len)
    bias = jnp.where(valid, 0.0, NEG_INF).astype(F32)
    lane = lax.broadcasted_iota(jnp.int32, (ATT_TQ, LANES), 1)
    lse_all = jnp.zeros((ATT_TQ, LANES), F32)
    for h in range(HEADS):
        sl = slice(h * HEAD_DIM, (h + 1) * HEAD_DIM)
        s = lax.dot_general(q_ref[0, 0, :, sl], kbuf[:, sl], (((1,), (1,)), ((), ())),
                            preferred_element_type=F32) + bias
        m = jnp.max(s, axis=-1, keepdims=True)
        p = jnp.exp(s - m)
        den = jnp.sum(p, axis=-1, keepdims=True)
        o = jnp.dot(p.astype(BF16), vbuf[:, sl], preferred_element_type=F32) / den
        o_ref[0, 0, :, sl] = o.astype(BF16)
        lse_all = jnp.where(lane == h, m + jnp.log(den), lse_all)
    lse_ref[0, 0] = lse_all


def _attention_group(qkv, dil):
    seq_len = SEQ // dil
    tq = ATT_TQ
    sub = tq // RADIUS
    n_sub = seq_len // RADIUS

    def cur(which):
        return pl.BlockSpec((1, 1, tq, D_MODEL), lambda b, r, i: (b, r, i, which))

    def prev(which):
        return pl.BlockSpec((1, 1, RADIUS, D_MODEL), lambda b, r, i: (b, r, jnp.maximum(i * sub - 1, 0), which))

    def nxt(which):
        return pl.BlockSpec((1, 1, RADIUS, D_MODEL),
                            lambda b, r, i: (b, r, jnp.minimum((i + 1) * sub, n_sub - 1), which))

    return pl.pallas_call(
        functools.partial(_attn_kernel, seq_len=seq_len),
        grid=(BATCH, dil, seq_len // tq),
        in_specs=[cur(0), prev(1), cur(1), nxt(1), prev(2), cur(2), nxt(2)],
        out_specs=[pl.BlockSpec((1, 1, tq, D_MODEL), lambda b, r, i: (b, r, i, 0)),
                   pl.BlockSpec((1, 1, tq, LANES), lambda b, r, i: (b, r, i, 0))],
        out_shape=[jax.ShapeDtypeStruct((BATCH, dil, seq_len, D_MODEL), BF16),
                   jax.ShapeDtypeStruct((BATCH, dil, seq_len, LANES), F32)],
        scratch_shapes=[pltpu.VMEM((ATT_TK, D_MODEL), BF16), pltpu.VMEM((ATT_TK, D_MODEL), BF16)],
        compiler_params=_cparams(("arbitrary", "arbitrary", "arbitrary")),
        name=f"attn_dil{dil}",
    )(qkv, qkv, qkv, qkv, qkv, qkv, qkv)


def _deepnorm(x, y, gate, g, b):
    z = ALPHA * x + gate * y
    mu = jnp.mean(z, axis=-1, keepdims=True)
    zc = z - mu
    var = jnp.mean(zc * zc, axis=-1, keepdims=True)
    return zc * lax.rsqrt(var + LN_EPS) * g + b


def _split_bf16(a):
    hi = a.astype(BF16)
    lo = (a - hi.astype(F32)).astype(BF16)
    return hi, lo


def _router_logits(h, wr, br):
    h_hi, h_lo = _split_bf16(h)
    w_hi, w_lo = _split_bf16(wr)
    acc = jnp.dot(h_hi, w_hi, preferred_element_type=F32)
    acc = acc + jnp.dot(h_hi, w_lo, preferred_element_type=F32)
    acc = acc + jnp.dot(h_lo, w_hi, preferred_element_type=F32)
    return acc + br


def _store_slabs(ref, val):
    tm = val.shape[0]
    for s in range(SLABS):
        lo = val[:, (2 * s) * LANES:(2 * s + 1) * LANES].astype(BF16).astype(F32)
        hi = val[:, (2 * s + 1) * LANES:(2 * s + 2) * LANES].astype(BF16).astype(F32)
        word = ((lax.bitcast_convert_type(lo, U32) >> 16)
                | (lax.bitcast_convert_type(hi, U32) & jnp.uint32(0xFFFF0000)))
        ref[pl.ds(s, tm, stride=SLABS), :] = word


def _load_slabs(ref, base, tm):
    parts = []
    for s in range(SLABS):
        word = ref[pl.ds(base + s, tm, stride=SLABS), :]
        parts.append(lax.bitcast_convert_type(word << 16, F32))
        parts.append(lax.bitcast_convert_type(word & jnp.uint32(0xFFFF0000), F32))
    return jnp.concatenate(parts, axis=1)


PROJ_TM = 256


def _proj_kernel(*refs, merge):
    if merge:
        (o0, o1, o2, l0, l1, l2, w_ref, x_ref, gate_ref, g_ref, b_ref, sc_ref, sh_ref, wr_ref, br_ref,
         xn_ref, hs_ref, lg_ref, on1, on2, ln1, ln2) = refs
        for src, dst, dil in ((o1, on1, DILATIONS[1]), (o2, on2, DILATIONS[2])):
            for r in range(dil):
                for h in range(HEADS):
                    dst[h, pl.ds(r, PROJ_TM // dil, stride=dil), :] = (
                        src[0, r, :, h * HEAD_DIM:(h + 1) * HEAD_DIM].astype(F32))
        for src, dst, dil in ((l1, ln1, DILATIONS[1]), (l2, ln2, DILATIONS[2])):
            for r in range(dil):
                dst[pl.ds(r, PROJ_TM // dil, stride=dil), :] = src[0, r]
        ls = [l0[0, 0], ln1[...], ln2[...]]
        mx = jnp.maximum(jnp.maximum(ls[0], ls[1]), ls[2])
        es = [jnp.exp(l - mx) for l in ls]
        inv = 1.0 / (es[0] + es[1] + es[2])
        ws = [e * inv for e in es]
        parts = []
        for h in range(HEADS):
            sl = slice(h * HEAD_DIM, (h + 1) * HEAD_DIM)
            acc = ws[0][:, h:h + 1] * o0[0, 0, :, sl].astype(F32)
            acc = acc + ws[1][:, h:h + 1] * on1[h]
            acc = acc + ws[2][:, h:h + 1] * on2[h]
            parts.append(acc.astype(BF16))
        y_in = jnp.concatenate(parts, axis=1)
    else:
        (y_ref, w_ref, x_ref, gate_ref, g_ref, b_ref, sc_ref, sh_ref, wr_ref, br_ref,
         xn_ref, hs_ref, lg_ref) = refs
        y_in = y_ref[...]
    y = jnp.dot(y_in, w_ref[...], preferred_element_type=F32)
    xn = _deepnorm(x_ref[...], y, gate_ref[0], g_ref[...], b_ref[...])
    xn_ref[...] = xn
    h = xn * (1.0 + sc_ref[0]) + sh_ref[0]
    _store_slabs(hs_ref, h)
    lg_ref[...] = _router_logits(h, wr_ref[...], br_ref[...])


def _proj_ln_router(mix_in, w_bf, x2, gate, ln_g, ln_b, scale, shift, wr, br, *, merge):
    tm = PROJ_TM if merge else 2 * PROJ_TM
    per_b = SEQ // tm
    row = lambda i: (i, 0)
    full = lambda i: (0, 0)
    per_batch = lambda i: (i // per_b, 0, 0)
    scratch = []
    if merge:
        def resid(width, dil):
            return pl.BlockSpec((1, dil, tm // dil, width), lambda i: (i // per_b, 0, i % per_b, 0))
        mix_specs = ([resid(D_MODEL, dil) for dil in DILATIONS] + [resid(LANES, dil) for dil in DILATIONS])
        scratch = [pltpu.VMEM((HEADS, tm, HEAD_DIM), F32), pltpu.VMEM((HEADS, tm, HEAD_DIM), F32),
                   pltpu.VMEM((tm, LANES), F32), pltpu.VMEM((tm, LANES), F32)]
    else:
        mix_specs = [pl.BlockSpec((tm, D_MODEL), row)]
    return pl.pallas_call(
        functools.partial(_proj_kernel, merge=merge),
        grid=(TOKENS // tm,),
        in_specs=mix_specs + [
            pl.BlockSpec((D_MODEL, D_MODEL), full),
            pl.BlockSpec((tm, D_MODEL), row),
            pl.BlockSpec((1, 1, D_MODEL), per_batch),
            pl.BlockSpec((1, D_MODEL), full),
            pl.BlockSpec((1, D_MODEL), full),
            pl.BlockSpec((1, 1, D_MODEL), per_batch),
            pl.BlockSpec((1, 1, D_MODEL), per_batch),
            pl.BlockSpec((D_MODEL, LANES), full),
            pl.BlockSpec((1, LANES), full)],
        out_specs=[pl.BlockSpec((tm, D_MODEL), row),
                   pl.BlockSpec((tm * SLABS, LANES), row),
                   pl.BlockSpec((tm, LANES), row)],
        out_shape=[jax.ShapeDtypeStruct((TOKENS, D_MODEL), F32),
                   jax.ShapeDtypeStruct((TOKENS * SLABS, LANES), U32),
                   jax.ShapeDtypeStruct((TOKENS, LANES), F32)],
        scratch_shapes=scratch,
        compiler_params=_cparams(("arbitrary",)),
        name="proj_merge" if merge else "proj",
    )(*mix_in, w_bf, x2, gate, ln_g, ln_b, scale, shift, wr, br)


def _slab_rows(row):
    if isinstance(row, int):
        return pl.ds(row * SLABS, SLABS)
    return pl.ds(pl.multiple_of(row * SLABS, SLABS), SLABS)


def _row_copy(src, dst, s_row, d_row, sem):
    return pltpu.make_async_copy(src.at[_slab_rows(s_row)], dst.at[_slab_rows(d_row)], sem)


MOE_SLOTS = 3


def _moe_kernel(be_ref, nv_ref, rt_ref, hs_hbm, wg_ref, wu_ref, wd_ref, ys_ref, xbuf, wg_bf, wu_bf, wd_bf, sems):
    s = pl.program_id(0)
    blk = s - 1

    def gather(b):
        slot = b % MOE_SLOTS

        def issue(k, carry):
            _row_copy(hs_hbm, xbuf, rt_ref[b * MOE_BLK + k], slot * MOE_BLK + k, sems.at[slot]).start()
            return carry
        lax.fori_loop(0, MOE_BLK, issue, 0, unroll=8)

    def used(b):
        return (b < MOE_NB) & (nv_ref[jnp.minimum(b, MOE_NB - 1)] > 0)

    @pl.when(s == 0)
    def _():
        for b in range(MOE_SLOTS - 1):
            @pl.when(used(b))
            def _():
                gather(b)

    ahead = blk + MOE_SLOTS - 1

    @pl.when((s > 0) & used(ahead))
    def _():
        gather(ahead)

    nv = nv_ref[jnp.maximum(blk, 0)]

    @pl.when((s > 0) & (nv > 0))
    def _():
        slot = blk % MOE_SLOTS

        def drain(k, carry):
            _row_copy(hs_hbm, xbuf, 0, 0, sems.at[slot]).wait()
            return carry
        lax.fori_loop(0, MOE_BLK, drain, 0, unroll=8)

        x = _load_slabs(xbuf, pl.multiple_of(slot * (MOE_BLK * SLABS), MOE_BLK * SLABS), MOE_BLK)
        rows = lax.broadcasted_iota(jnp.int32, (MOE_BLK, 1), 0)
        x = jnp.where(rows < nv, x, 0.0).astype(BF16)
        g = jnp.dot(x, wg_bf[...], preferred_element_type=F32)
        u = jnp.dot(x, wu_bf[...], preferred_element_type=F32)
        a = (g * (1.0 / (1.0 + jnp.exp(-g)))) * u
        y = jnp.dot(a.astype(BF16), wd_bf[...], preferred_element_type=F32)
        _store_slabs(ys_ref, y)

    @pl.when((s > 0) & (nv <= 0))
    def _():
        ys_ref[...] = jnp.zeros_like(ys_ref)

    new_expert = (s == 0) | (be_ref[jnp.minimum(s, MOE_NB - 1)] != be_ref[jnp.maximum(blk, 0)])

    @pl.when(used(s) & new_expert)
    def _():
        wg_bf[...] = wg_ref[0, 0].astype(BF16)
        wu_bf[...] = wu_ref[0, 0].astype(BF16)
        wd_bf[...] = wd_ref[0, 0].astype(BF16)


def _moe_ffn(block_e, block_nv, row_tok, hs, w_gate, w_up, w_down, layer):
    wmap = lambda s, be, nv, rt: (layer, be[jnp.minimum(s, MOE_NB - 1)], 0, 0)
    return pl.pallas_call(
        _moe_kernel,
        grid_spec=pltpu.PrefetchScalarGridSpec(
            num_scalar_prefetch=3,
            grid=(MOE_NB + 1,),
            in_specs=[pl.BlockSpec(memory_space=pl.ANY),
                      pl.BlockSpec((1, 1, D_MODEL, EXPERT_DIM), wmap),
                      pl.BlockSpec((1, 1, D_MODEL, EXPERT_DIM), wmap),
                      pl.BlockSpec((1, 1, EXPERT_DIM, D_MODEL), wmap)],
            out_specs=pl.BlockSpec((MOE_BLK * SLABS, LANES), lambda s, be, nv, rt: (jnp.maximum(s - 1, 0), 0)),
            scratch_shapes=[pltpu.VMEM((MOE_SLOTS * MOE_BLK * SLABS, LANES), U32),
                            pltpu.VMEM((D_MODEL, EXPERT_DIM), BF16),
                            pltpu.VMEM((D_MODEL, EXPERT_DIM), BF16),
                            pltpu.VMEM((EXPERT_DIM, D_MODEL), BF16),
                            pltpu.SemaphoreType.DMA((MOE_SLOTS,))]),
        out_shape=jax.ShapeDtypeStruct((MOE_ROWS * SLABS, LANES), U32),
        compiler_params=pltpu.CompilerParams(dimension_semantics=("arbitrary",), vmem_limit_bytes=VMEM_LIMIT,
                                             disable_bounds_checks=True),
        name="moe_ffn",
    )(block_e, block_nv, row_tok, hs, w_gate, w_up, w_down)


COMB_TM = 256


def _combine_kernel(dest_ref, ys_hbm, gt_ref, x_ref, gate_ref, g_ref, b_ref, *rest, modulate):
    if modulate:
        sc_ref, sh_ref, xn_ref, hn_ref, buf, sems = rest
    else:
        xn_ref, buf, sems = rest
    i = pl.program_id(0)
    tm = COMB_TM

    def gather(tile, slot):
        def issue(t, carry):
            tok = tile * tm + t
            for k in range(2):
                _row_copy(ys_hbm, buf, dest_ref[2 * tok + k], (2 * slot + k) * tm + t, sems.at[slot]).start()
            return carry
        lax.fori_loop(0, tm, issue, 0, unroll=4)

    @pl.when(i == 0)
    def _():
        gather(0, 0)

    @pl.when(i + 1 < pl.num_programs(0))
    def _():
        gather(i + 1, (i + 1) % 2)

    slot = i % 2

    def drain(t, carry):
        _row_copy(ys_hbm, buf, 0, 0, sems.at[slot]).wait()
        _row_copy(ys_hbm, buf, 0, 0, sems.at[slot]).wait()
        return carry

    lax.fori_loop(0, tm, drain, 0, unroll=4)

    gt = gt_ref[...]
    base = pl.multiple_of(slot * (2 * tm * SLABS), 2 * tm * SLABS)
    y = _load_slabs(buf, base, tm) * gt[:, 0:1] + _load_slabs(buf, base + tm * SLABS, tm) * gt[:, 1:2]
    xn = _deepnorm(x_ref[...], y, gate_ref[0], g_ref[...], b_ref[...])
    xn_ref[...] = xn
    if modulate:
        hn_ref[...] = (xn * (1.0 + sc_ref[0]) + sh_ref[0]).astype(BF16)


def _combine(dest, ys, gates, x2, gate, ln_g, ln_b, scale=None, shift=None):
    tm = COMB_TM
    per_b = SEQ // tm
    modulate = scale is not None
    row = lambda i, d: (i, 0)
    full = lambda i, d: (0, 0)
    per_batch = lambda i, d: (i // per_b, 0, 0)
    in_specs = [pl.BlockSpec(memory_space=pl.ANY),
                pl.BlockSpec((tm, 2), row),
                pl.BlockSpec((tm, D_MODEL), row),
                pl.BlockSpec((1, 1, D_MODEL), per_batch),
                pl.BlockSpec((1, D_MODEL), full),
                pl.BlockSpec((1, D_MODEL), full)]
    out_specs = [pl.BlockSpec((tm, D_MODEL), row)]
    out_shape = [jax.ShapeDtypeStruct((TOKENS, D_MODEL), F32)]
    args = [dest, ys, gates, x2, gate, ln_g, ln_b]
    if modulate:
        in_specs += [pl.BlockSpec((1, 1, D_MODEL), per_batch)] * 2
        out_specs.append(pl.BlockSpec((tm, D_MODEL), row))
        out_shape.append(jax.ShapeDtypeStruct((TOKENS, D_MODEL), BF16))
        args += [scale, shift]
    return pl.pallas_call(
        functools.partial(_combine_kernel, modulate=modulate),
        grid_spec=pltpu.PrefetchScalarGridSpec(
            num_scalar_prefetch=1,
            grid=(TOKENS // tm,),
            in_specs=in_specs,
            out_specs=out_specs,
            scratch_shapes=[pltpu.VMEM((4 * tm * SLABS, LANES), U32), pltpu.SemaphoreType.DMA((2,))]),
        out_shape=out_shape,
        compiler_params=pltpu.CompilerParams(dimension_semantics=("arbitrary",), vmem_limit_bytes=VMEM_LIMIT,
                                             disable_bounds_checks=True),
        name="moe_combine",
    )(*args)


def _route(logits):
    lc = logits[:, :N_GROUPS]
    lf = logits[:, N_GROUPS:N_GROUPS + N_EXPERTS].reshape(TOKENS, N_GROUPS, EPG)
    pc = jax.nn.softmax(lc, axis=-1)
    pg = jnp.max(pc, axis=-1, keepdims=True)
    gi = jnp.argmax(pc, axis=-1)[:, None]
    lf_sel = jnp.take_along_axis(lf, gi[:, :, None], axis=1)[:, 0]
    pf = jax.nn.softmax(lf_sel, axis=-1)
    e1 = jnp.argmax(pf, axis=-1)[:, None]
    p1 = jnp.max(pf, axis=-1, keepdims=True)
    pf2 = jnp.where(jnp.arange(EPG)[None, :] == e1, -1.0, pf)
    e2 = jnp.argmax(pf2, axis=-1)[:, None]
    p2 = jnp.max(pf2, axis=-1, keepdims=True)
    pk = jnp.concatenate([p1, p2], axis=1)
    ek = jnp.concatenate([e1, e2], axis=1)
    gates = pg * (pk / jnp.sum(pk, axis=-1, keepdims=True))
    e_flat = (gi * EPG + ek).reshape(-1).astype(jnp.int32)
    onehot = (e_flat[:, None] == jnp.arange(N_EXPERTS, dtype=jnp.int32)[None, :]).astype(jnp.int32)
    csum = jnp.cumsum(onehot, axis=0)
    rank = jnp.sum((csum - onehot) * onehot, axis=1)
    counts = csum[-1]
    padded = (counts + MOE_BLK - 1) // MOE_BLK * MOE_BLK
    pends = jnp.cumsum(padded)
    pstarts = pends - padded
    dest = (pstarts[e_flat] + rank).astype(jnp.int32)
    blk_start = jnp.arange(MOE_NB, dtype=jnp.int32) * MOE_BLK
    block_e = jnp.clip(jnp.searchsorted(pends, blk_start, side='right'), 0, N_EXPERTS - 1).astype(jnp.int32)
    block_nv = jnp.clip(counts[block_e] - (blk_start - pstarts[block_e]), 0, MOE_BLK)
    block_nv = jnp.where(blk_start < pends[-1], block_nv, 0).astype(jnp.int32)
    row_tok = jnp.zeros((MOE_ROWS,), jnp.int32).at[dest].set(jnp.arange(2 * TOKENS, dtype=jnp.int32) // 2)
    return gates, dest, row_tok, block_e, block_nv


def _moe(hs, logits, x2, gate, ln_g, ln_b, w_gate, w_up, w_down, layer, scale=None, shift=None):
    gates, dest, row_tok, block_e, block_nv = _route(logits)
    ys = _moe_ffn(block_e, block_nv, row_tok, hs, w_gate, w_up, w_down, layer)
    return _combine(dest, ys, gates, x2, gate, ln_g, ln_b, scale, shift)


FN_TM = 512


def _fnet_in_kernel(h_ref, w_ref, cs_ref, ab_ref):
    u = jnp.dot(h_ref[...], w_ref[...], preferred_element_type=F32).astype(BF16)
    for g in range(FNET_GROUPS):
        sl = slice(g * FNET_CH, (g + 1) * FNET_CH)
        ab = jnp.dot(u[:, sl], cs_ref[...], preferred_element_type=F32)
        ab_ref[0, 0, :, sl] = ab[:, :FNET_CH].astype(BF16)
        ab_ref[0, 1, :, sl] = ab[:, FNET_CH:].astype(BF16)


def _fnet_in(h_bf, w_in_bf, cs):
    tm = FN_TM
    per_b = SEQ // tm
    return pl.pallas_call(
        _fnet_in_kernel,
        grid=(TOKENS // tm,),
        in_specs=[pl.BlockSpec((tm, D_MODEL), lambda i: (i, 0)),
                  pl.BlockSpec((D_MODEL, D_MODEL), lambda i: (0, 0)),
                  pl.BlockSpec((FNET_CH, 2 * FNET_CH), lambda i: (0, 0))],
        out_specs=pl.BlockSpec((1, 2, tm, D_MODEL), lambda i: (i // per_b, 0, i % per_b, 0)),
        out_shape=jax.ShapeDtypeStruct((BATCH, 2, SEQ, D_MODEL), BF16),
        compiler_params=_cparams(("arbitrary",)),
        name="fnet_in",
    )(h_bf, w_in_bf, cs)


DFT_R = 4
DFT_Q = SEQ // DFT_R
DFT_TN = 256
DFT_CHUNK = 256
_C4 = (1, 0, -1, 0)
_S4 = (0, 1, 0, -1)


def _signed_sum(terms):
    acc = None
    for sign, v in terms:
        if sign == 0:
            continue
        if acc is None:
            acc = v if sign > 0 else -v
        else:
            acc = acc + v if sign > 0 else acc - v
    return acc


def _dft_kernel(a_ref, b_ref, m_hbm, y_ref, m_scr, ab_scr, y_scr, sem):
    @pl.when((pl.program_id(0) == 0) & (pl.program_id(1) == 0))
    def _():
        cp = pltpu.make_async_copy(m_hbm, m_scr, sem)
        cp.start()
        cp.wait()

    for ch in range(DFT_Q // DFT_CHUNK):
        rows = [slice(m * DFT_Q + ch * DFT_CHUNK, m * DFT_Q + (ch + 1) * DFT_CHUNK) for m in range(DFT_R)]
        a = [a_ref[0, 0, r, :].astype(F32) for r in rows]
        b = [b_ref[0, 0, r, :].astype(F32) for r in rows]
        for rho in range(DFT_R):
            q = [(rho * m) % 4 for m in range(DFT_R)]
            ap = _signed_sum([(_C4[q[m]], a[m]) for m in range(DFT_R)] + [(-_S4[q[m]], b[m]) for m in range(DFT_R)])
            bp = _signed_sum([(_S4[q[m]], a[m]) for m in range(DFT_R)] + [(_C4[q[m]], b[m]) for m in range(DFT_R)])
            ab_scr[rho, ch * DFT_CHUNK:(ch + 1) * DFT_CHUNK, :] = ap.astype(BF16)
            ab_scr[rho, DFT_Q + ch * DFT_CHUNK:DFT_Q + (ch + 1) * DFT_CHUNK, :] = bp.astype(BF16)

    for rho in range(DFT_R):
        y = jnp.dot(m_scr[rho], ab_scr[rho], preferred_element_type=F32)
        for c in range(DFT_TN // LANES):
            y_scr[c, pl.ds(rho, DFT_Q, stride=DFT_R), :] = y[:, c * LANES:(c + 1) * LANES]
    for c in range(DFT_TN // LANES):
        y_ref[0, :, c * LANES:(c + 1) * LANES] = y_scr[c].astype(BF16)


def _seq_dft(mats, ab):
    tn = DFT_TN
    return pl.pallas_call(
        _dft_kernel,
        grid=(BATCH, D_MODEL // tn),
        in_specs=[pl.BlockSpec((1, 1, SEQ, tn), lambda b, j: (b, 0, 0, j)),
                  pl.BlockSpec((1, 1, SEQ, tn), lambda b, j: (b, 1, 0, j)),
                  pl.BlockSpec(memory_space=pl.ANY)],
        out_specs=pl.BlockSpec((1, SEQ, tn), lambda b, j: (b, 0, j)),
        out_shape=jax.ShapeDtypeStruct((BATCH, SEQ, D_MODEL), BF16),
        scratch_shapes=[pltpu.VMEM((DFT_R, DFT_Q, 2 * DFT_Q), BF16),
                        pltpu.VMEM((DFT_R, 2 * DFT_Q, tn), BF16),
                        pltpu.VMEM((tn // LANES, SEQ, LANES), F32),
                        pltpu.SemaphoreType.DMA(())],
        compiler_params=_cparams(("arbitrary", "arbitrary")),
        name="seq_dft",
    )(ab, ab, mats)


def _dft_constants():
    n = np.arange(FNET_CH)
    ang = 2.0 * np.pi * ((n[:, None] * n[None, :]) % FNET_CH) / FNET_CH
    cs = np.concatenate([np.cos(ang), np.sin(ang)], axis=1) / math.sqrt(FNET_CH)
    return jnp.asarray(cs, dtype=BF16)


def _seq_dft_matrices():
    shape = (DFT_R, DFT_Q, DFT_Q)
    rho = lax.broadcasted_iota(jnp.int32, shape, 0)
    kp = lax.broadcasted_iota(jnp.int32, shape, 1)
    jp = lax.broadcasted_iota(jnp.int32, shape, 2)
    ang = (((DFT_R * kp + rho) * jp) & (SEQ - 1)).astype(F32) * (2.0 * math.pi / SEQ)
    norm = 1.0 / math.sqrt(SEQ)
    return jnp.concatenate([(jnp.cos(ang) * norm).astype(BF16), (jnp.sin(ang) * (-norm)).astype(BF16)], axis=2)


def kernel(x, c, positions, ada_w, ada_b, attn_w_qkv, attn_w_o, fnet_w_in, fnet_w_out, ln_g, ln_b,
           router_coarse_w, router_coarse_b, router_fine_w, router_fine_b,
           expert_w_gate, expert_w_up, expert_w_down):
    x2 = x.reshape(TOKENS, D_MODEL)

    c8 = jnp.zeros((8, D_MODEL), F32).at[:BATCH].set(c)
    mod = _adaln(c8, ada_w.reshape(4, D_MODEL, 3 * D_MODEL), ada_b.reshape(4, 1, 3 * D_MODEL))[:, :BATCH]

    def mod_rows(idx):
        m = mod[idx].reshape(BATCH, 3, 1, D_MODEL)
        return m[:, 0], m[:, 1], m[:, 2]

    def router_params(i):
        wr = jnp.zeros((D_MODEL, LANES), F32)
        wr = wr.at[:, :N_GROUPS].set(router_coarse_w[i]).at[:, N_GROUPS:N_GROUPS + N_EXPERTS].set(router_fine_w[i])
        br = jnp.zeros((1, LANES), F32)
        br = br.at[0, :N_GROUPS].set(router_coarse_b[i]).at[0, N_GROUPS:N_GROUPS + N_EXPERTS].set(router_fine_b[i])
        return wr, br

    inv_freq = np.float32(ROPE_THETA) ** (-np.arange(0, ROT_DIM, 2, dtype=np.float32) / np.float32(ROT_DIM))
    invf = np.zeros((1, LANES), np.float32)
    invf[0, :ROT_HALF] = inv_freq
    invf[0, ROT_HALF:ROT_DIM] = inv_freq
    pos_rm = [positions.reshape(BATCH, SEQ // QKV_TM, QKV_TM // dil, dil).transpose(0, 1, 3, 2).reshape(TOKENS)
              for dil in DILATIONS]
    tabs = _rot_tables(jnp.concatenate(pos_rm).reshape(N_DIL * TOKENS, 1), jnp.asarray(invf))

    shift, scale, gate = mod_rows(0)
    outs, lses = [], []
    for g, dil in enumerate(DILATIONS):
        o_g, lse_g = _attention_group(_qkv(x2, scale, shift, attn_w_qkv, tabs, g, dil), dil)
        outs.append(o_g)
        lses.append(lse_g)
    shift2, scale2, gate2 = mod_rows(1)
    wr, br = router_params(0)
    x2, hs, logits = _proj_ln_router(outs + lses, attn_w_o[0].astype(BF16), x2, gate, ln_g[0, 0][None], ln_b[0, 0][None],
                                     scale2, shift2, wr, br, merge=True)
    shift3, scale3, gate3 = mod_rows(2)
    x2, h_bf = _moe(hs, logits, x2, gate2, ln_g[0, 1][None], ln_b[0, 1][None],
                    expert_w_gate, expert_w_up, expert_w_down, 0, scale3, shift3)

    ab = _fnet_in(h_bf, fnet_w_in[0].astype(BF16), _dft_constants())
    y = _seq_dft(_seq_dft_matrices(), ab).reshape(TOKENS, D_MODEL)
    shift4, scale4, gate4 = mod_rows(3)
    wr, br = router_params(1)
    x2, hs, logits = _proj_ln_router([y], fnet_w_out[0].astype(BF16), x2, gate3, ln_g[1, 0][None], ln_b[1, 0][None],
                                     scale4, shift4, wr, br, merge=False)
    (x2,) = _moe(hs, logits, x2, gate4, ln_g[1, 1][None], ln_b[1, 1][None],
                 expert_w_gate, expert_w_up, expert_w_down, 1)
    return x2.reshape(BATCH, SEQ, D_MODEL)
```

```python
import functools
import math

import numpy as np
import jax
import jax.numpy as jnp
from jax import lax
from jax.experimental import pallas as pl
from jax.experimental.pallas import tpu as pltpu

F32 = jnp.float32
BF16 = jnp.bfloat16
U32 = jnp.uint32

D_MODEL = 2048
BATCH = 2
SEQ = 4096
TOKENS = BATCH * SEQ
DEPTH = 2
HEAD_DIM = 128
HEADS = D_MODEL // HEAD_DIM
DILATIONS = (1, 4, 16)
RADIUS = 64
N_DIL = 3
QKV_COLS = N_DIL * 3 * D_MODEL
ROT_DIM = HEAD_DIM // 4
ROT_HALF = ROT_DIM // 2
ROPE_THETA = 500000.0
FNET_GROUPS = 4
FNET_CH = D_MODEL // FNET_GROUPS
N_GROUPS = 4
EPG = 8
N_EXPERTS = N_GROUPS * EPG
EXPERT_DIM = D_MODEL // 4
LN_EPS = 1e-5
NEG_INF = -1e30
ALPHA = (2 * DEPTH) ** 0.25

LANES = 128
SLABS = D_MODEL // (2 * LANES)
VMEM_LIMIT = 56 * 1024 * 1024

MOE_BLK = 256
MOE_NB = (2 * TOKENS) // MOE_BLK + N_EXPERTS
MOE_ROWS = MOE_NB * MOE_BLK


def _cparams(sem):
    return pltpu.CompilerParams(dimension_semantics=sem, vmem_limit_bytes=VMEM_LIMIT)


def _adaln_kernel(c_ref, w_ref, b_ref, o_ref):
    c = c_ref[...]
    sc = (c * (1.0 / (1.0 + jnp.exp(-c)))).astype(BF16)
    m = jnp.dot(sc, w_ref[0].astype(BF16), preferred_element_type=F32)
    o_ref[0] = m + b_ref[0]


def _adaln(c8, w4, b4):
    tn = 768
    n = 3 * D_MODEL
    return pl.pallas_call(
        _adaln_kernel,
        grid=(4, n // tn),
        in_specs=[pl.BlockSpec((8, D_MODEL), lambda s, j: (0, 0)),
                  pl.BlockSpec((1, D_MODEL, tn), lambda s, j: (s, 0, j)),
                  pl.BlockSpec((1, 1, tn), lambda s, j: (s, 0, j))],
        out_specs=pl.BlockSpec((1, 8, tn), lambda s, j: (s, 0, j)),
        out_shape=jax.ShapeDtypeStruct((4, 8, n), F32),
        compiler_params=_cparams(("arbitrary", "arbitrary")),
        name="adaln",
    )(c8, w4, b4)


def _rot_tab_kernel(pos_ref, invf_ref, o_ref):
    ang = pos_ref[...].astype(F32) * invf_ref[...]
    lane = lax.broadcasted_iota(jnp.int32, ang.shape, 1)
    c = jnp.where(lane < ROT_DIM, jnp.cos(ang), 1.0)
    s = jnp.sin(ang)
    s_lo = jnp.where(lane < ROT_HALF, -s, 0.0)
    s_hi = jnp.where((lane >= ROT_HALF) & (lane < ROT_DIM), s, 0.0)
    qs = HEAD_DIM ** -0.5
    o_ref[0] = c * qs
    o_ref[1] = s_lo * qs
    o_ref[2] = s_hi * qs
    o_ref[3] = c
    o_ref[4] = s_lo
    o_ref[5] = s_hi
    o_ref[6] = jnp.ones_like(c)
    o_ref[7] = jnp.zeros_like(c)
    o_ref[8] = jnp.zeros_like(c)


def _rot_tables(pos_col, invf):
    tm = 1024
    n = pos_col.shape[0]
    return pl.pallas_call(
        _rot_tab_kernel,
        grid=(n // tm,),
        in_specs=[pl.BlockSpec((tm, 1), lambda i: (i, 0)),
                  pl.BlockSpec((1, LANES), lambda i: (0, 0))],
        out_specs=pl.BlockSpec((9, tm, LANES), lambda i: (0, i, 0)),
        out_shape=jax.ShapeDtypeStruct((9, n, LANES), F32),
        compiler_params=_cparams(("arbitrary",)),
        name="rot_tables",
    )(pos_col, invf)


QKV_TM = 1024
QKV_TN = 512


def _rotate(t, c, s_lo, s_hi):
    return (t * c + pltpu.roll(t, LANES - ROT_HALF, axis=1) * s_lo
            + pltpu.roll(t, ROT_HALF, axis=1) * s_hi)


QKV_NI = TOKENS // QKV_TM
QKV_NJ = 3 * D_MODEL // QKV_TN
QKV_TILES = QKV_NI * QKV_NJ


def _qkv_kernel(x_ref, sc_ref, sh_ref, w_ref, tab_ref, o_ref, h_ref, acc_a, acc_b, *, dil):
    g = pl.program_id(0)

    @pl.when(g % QKV_NJ == 0)
    def _():
        h_ref[...] = (x_ref[...] * (1.0 + sc_ref[0]) + sh_ref[0]).astype(BF16)

    @pl.when(g == 0)
    def _():
        acc_b[...] = jnp.zeros_like(acc_b)

    rows = QKV_TM // dil

    def step(read_ref, write_ref):
        acc = jnp.dot(h_ref[...], w_ref[0].astype(BF16), preferred_element_type=F32)
        for hc in range(QKV_TN // LANES):
            write_ref[hc] = acc[:, hc * LANES:(hc + 1) * LANES]
        for hc in range(QKV_TN // LANES):
            sl = slice(hc * LANES, (hc + 1) * LANES)
            for r in range(dil):
                rs = slice(r * rows, (r + 1) * rows)
                t = read_ref[hc] if dil == 1 else read_ref[hc, pl.ds(r, rows, stride=dil), :]
                o_ref[0, r, :, sl] = _rotate(t, tab_ref[0, rs], tab_ref[1, rs], tab_ref[2, rs]).astype(BF16)

    @pl.when(g % 2 == 0)
    def _():
        step(acc_b, acc_a)

    @pl.when(g % 2 == 1)
    def _():
        step(acc_a, acc_b)


def _qkv(x2, scale, shift, w_qkv, tabs, grp, dil):
    tm, tn = QKV_TM, QKV_TN
    per_b = SEQ // tm

    def cur(g):
        t = jnp.minimum(g, QKV_TILES - 1)
        return t // QKV_NJ, t % QKV_NJ

    def prv(g):
        t = jnp.maximum(g - 1, 0)
        return t // QKV_NJ, t % QKV_NJ

    return pl.pallas_call(
        functools.partial(_qkv_kernel, dil=dil),
        grid=(QKV_TILES + 1,),
        in_specs=[pl.BlockSpec((tm, D_MODEL), lambda g: (cur(g)[0], 0)),
                  pl.BlockSpec((1, 1, D_MODEL), lambda g: (cur(g)[0] // per_b, 0, 0)),
                  pl.BlockSpec((1, 1, D_MODEL), lambda g: (cur(g)[0] // per_b, 0, 0)),
                  pl.BlockSpec((1, D_MODEL, tn), lambda g: (0, 0, grp * QKV_NJ + cur(g)[1])),
                  pl.BlockSpec((3, tm, LANES),
                               lambda g: (prv(g)[1] // (D_MODEL // tn), grp * QKV_NI + prv(g)[0], 0))],
        out_specs=pl.BlockSpec((1, dil, tm // dil, tn),
                               lambda g: (prv(g)[0] // per_b, 0, prv(g)[0] % per_b, prv(g)[1])),
        out_shape=jax.ShapeDtypeStruct((BATCH, dil, SEQ // dil, 3 * D_MODEL), BF16),
        scratch_shapes=[pltpu.VMEM((tm, D_MODEL), BF16),
                        pltpu.VMEM((tn // LANES, tm, LANES), F32),
                        pltpu.VMEM((tn // LANES, tm, LANES), F32)],
        compiler_params=_cparams(("arbitrary",)),
        name=f"qkv_proj_dil{dil}",
    )(x2, scale, shift, w_qkv, tabs)


ATT_TQ = 128
ATT_TK = ATT_TQ + 2 * RADIUS


def _attn_kernel(q_ref, kp_ref, kc_ref, kn_ref, vp_ref, vc_ref, vn_ref, o_ref, lse_ref, kbuf, vbuf, *, seq_len):
    i = pl.program_id(2)
    kbuf[0:RADIUS] = kp_ref[0, 0]
    kbuf[RADIUS:RADIUS + ATT_TQ] = kc_ref[0, 0]
    kbuf[RADIUS + ATT_TQ:ATT_TK] = kn_ref[0, 0]
    vbuf[0:RADIUS] = vp_ref[0, 0]
    vbuf[RADIUS:RADIUS + ATT_TQ] = vc_ref[0, 0]
    vbuf[RADIUS + ATT_TQ:ATT_TK] = vn_ref[0, 0]

    qpos = i * ATT_TQ + lax.broadcasted_iota(jnp.int32, (ATT_TQ, ATT_TK), 0)
    kpos = i * ATT_TQ - RADIUS + lax.broadcasted_iota(jnp.int32, (ATT_TQ, ATT_TK), 1)
    valid = (jnp.abs(qpos - kpos) <= RADIUS) & (kpos >= 0) & (kpos < seq_len)
    bias = jnp.where(valid, 0.0, NEG_INF).astype(F32)
    lane = lax.broadcasted_iota(jnp.int32, (ATT_TQ, LANES), 1)
    lse_all = jnp.zeros((ATT_TQ, LANES), F32)
    for h in range(HEADS):
        sl = slice(h * HEAD_DIM, (h + 1) * HEAD_DIM)
        s = lax.dot_general(q_ref[0, 0, :, sl], kbuf[:, sl], (((1,), (1,)), ((), ())),
                            preferred_element_type=F32) + bias
        m = jnp.max(s, axis=-1, keepdims=True)
        p = jnp.exp(s - m)
        den = jnp.sum(p, axis=-1, keepdims=True)
        o = jnp.dot(p.astype(BF16), vbuf[:, sl], preferred_element_type=F32) / den
        o_ref[0, 0, :, sl] = o.astype(BF16)
        lse_all = jnp.where(lane == h, m + jnp.log(den), lse_all)
    lse_ref[0, 0] = lse_all


def _attention_group(qkv, dil):
    seq_len = SEQ // dil
    tq = ATT_TQ
    sub = tq // RADIUS
    n_sub = seq_len // RADIUS

    def cur(which):
        return pl.BlockSpec((1, 1, tq, D_MODEL), lambda b, r, i: (b, r, i, which))

    def prev(which):
        return pl.BlockSpec((1, 1, RADIUS, D_MODEL), lambda b, r, i: (b, r, jnp.maximum(i * sub - 1, 0), which))

    def nxt(which):
        return pl.BlockSpec((1, 1, RADIUS, D_MODEL),
                            lambda b, r, i: (b, r, jnp.minimum((i + 1) * sub, n_sub - 1), which))

    return pl.pallas_call(
        functools.partial(_attn_kernel, seq_len=seq_len),
        grid=(BATCH, dil, seq_len // tq),
        in_specs=[cur(0), prev(1), cur(1), nxt(1), prev(2), cur(2), nxt(2)],
        out_specs=[pl.BlockSpec((1, 1, tq, D_MODEL), lambda b, r, i: (b, r, i, 0)),
                   pl.BlockSpec((1, 1, tq, LANES), lambda b, r, i: (b, r, i, 0))],
        out_shape=[jax.ShapeDtypeStruct((BATCH, dil, seq_len, D_MODEL), BF16),
                   jax.ShapeDtypeStruct((BATCH, dil, seq_len, LANES), F32)],
        scratch_shapes=[pltpu.VMEM((ATT_TK, D_MODEL), BF16), pltpu.VMEM((ATT_TK, D_MODEL), BF16)],
        compiler_params=_cparams(("arbitrary", "arbitrary", "arbitrary")),
        name=f"attn_dil{dil}",
    )(qkv, qkv, qkv, qkv, qkv, qkv, qkv)


def _deepnorm(x, y, gate, g, b):
    z = ALPHA * x + gate * y
    mu = jnp.mean(z, axis=-1, keepdims=True)
    zc = z - mu
    var = jnp.mean(zc * zc, axis=-1, keepdims=True)
    return zc * lax.rsqrt(var + LN_EPS) * g + b


def _split_bf16(a):
    hi = a.astype(BF16)
    lo = (a - hi.astype(F32)).astype(BF16)
    return hi, lo


def _router_logits(h, wr, br):
    h_hi, h_lo = _split_bf16(h)
    w_hi, w_lo = _split_bf16(wr)
    acc = jnp.dot(h_hi, w_hi, preferred_element_type=F32)
    acc = acc + jnp.dot(h_hi, w_lo, preferred_element_type=F32)
    acc = acc + jnp.dot(h_lo, w_hi, preferred_element_type=F32)
    return acc + br


def _store_slabs(ref, val):
    tm = val.shape[0]
    for s in range(SLABS):
        lo = val[:, (2 * s) * LANES:(2 * s + 1) * LANES].astype(BF16).astype(F32)
        hi = val[:, (2 * s + 1) * LANES:(2 * s + 2) * LANES].astype(BF16).astype(F32)
        word = ((lax.bitcast_convert_type(lo, U32) >> 16)
                | (lax.bitcast_convert_type(hi, U32) & jnp.uint32(0xFFFF0000)))
        ref[pl.ds(s, tm, stride=SLABS), :] = word


def _load_slabs(ref, base, tm):
    parts = []
    for s in range(SLABS):
        word = ref[pl.ds(base + s, tm, stride=SLABS), :]
        parts.append(lax.bitcast_convert_type(word << 16, F32))
        parts.append(lax.bitcast_convert_type(word & jnp.uint32(0xFFFF0000), F32))
    return jnp.concatenate(parts, axis=1)


PROJ_TM = 256


def _proj_kernel(*refs, merge):
    if merge:
        (o0, o1, o2, l0, l1, l2, w_ref, x_ref, gate_ref, g_ref, b_ref, sc_ref, sh_ref, wr_ref, br_ref,
         xn_ref, hs_ref, lg_ref, on1, on2, ln1, ln2) = refs
        for src, dst, dil in ((o1, on1, DILATIONS[1]), (o2, on2, DILATIONS[2])):
            for r in range(dil):
                for h in range(HEADS):
                    dst[h, pl.ds(r, PROJ_TM // dil, stride=dil), :] = (
                        src[0, r, :, h * HEAD_DIM:(h + 1) * HEAD_DIM].astype(F32))
        for src, dst, dil in ((l1, ln1, DILATIONS[1]), (l2, ln2, DILATIONS[2])):
            for r in range(dil):
                dst[pl.ds(r, PROJ_TM // dil, stride=dil), :] = src[0, r]
        ls = [l0[0, 0], ln1[...], ln2[...]]
        mx = jnp.maximum(jnp.maximum(ls[0], ls[1]), ls[2])
        es = [jnp.exp(l - mx) for l in ls]
        inv = 1.0 / (es[0] + es[1] + es[2])
        ws = [e * inv for e in es]
        parts = []
        for h in range(HEADS):
            sl = slice(h * HEAD_DIM, (h + 1) * HEAD_DIM)
            acc = ws[0][:, h:h + 1] * o0[0, 0, :, sl].astype(F32)
            acc = acc + ws[1][:, h:h + 1] * on1[h]
            acc = acc + ws[2][:, h:h + 1] * on2[h]
            parts.append(acc.astype(BF16))
        y_in = jnp.concatenate(parts, axis=1)
    else:
        (y_ref, w_ref, x_ref, gate_ref, g_ref, b_ref, sc_ref, sh_ref, wr_ref, br_ref,
         xn_ref, hs_ref, lg_ref) = refs
        y_in = y_ref[...]
    y = jnp.dot(y_in, w_ref[...], preferred_element_type=F32)
    xn = _deepnorm(x_ref[...], y, gate_ref[0], g_ref[...], b_ref[...])
    xn_ref[...] = xn
    h = xn * (1.0 + sc_ref[0]) + sh_ref[0]
    _store_slabs(hs_ref, h)
    lg_ref[...] = _router_logits(h, wr_ref[...], br_ref[...])


def _proj_pipe_kernel(y_ref, w_ref, x_ref, gate_ref, g_ref, b_ref, sc_ref, sh_ref, wr_ref, br_ref,
                      xn_ref, hs_ref, lg_ref, acc_a, acc_b):
    g = pl.program_id(0)

    @pl.when(g == 0)
    def _():
        acc_b[...] = jnp.zeros_like(acc_b)

    def step(read_ref, write_ref):
        write_ref[...] = jnp.dot(y_ref[...], w_ref[...], preferred_element_type=F32)
        xn = _deepnorm(x_ref[...], read_ref[...], gate_ref[0], g_ref[...], b_ref[...])
        xn_ref[...] = xn
        h = xn * (1.0 + sc_ref[0]) + sh_ref[0]
        _store_slabs(hs_ref, h)
        lg_ref[...] = _router_logits(h, wr_ref[...], br_ref[...])

    @pl.when(g % 2 == 0)
    def _():
        step(acc_b, acc_a)

    @pl.when(g % 2 == 1)
    def _():
        step(acc_a, acc_b)


def _proj_ln_router_pipelined(y, w_bf, x2, gate, ln_g, ln_b, scale, shift, wr, br):
    tm = PROJ_TM
    n = TOKENS // tm
    per_b = SEQ // tm
    cur = lambda g: (jnp.minimum(g, n - 1), 0)
    prv = lambda g: (jnp.maximum(g - 1, 0), 0)
    full = lambda g: (0, 0)
    prv_batch = lambda g: (jnp.maximum(g - 1, 0) // per_b, 0, 0)
    return pl.pallas_call(
        _proj_pipe_kernel,
        grid=(n + 1,),
        in_specs=[pl.BlockSpec((tm, D_MODEL), cur),
                  pl.BlockSpec((D_MODEL, D_MODEL), full),
                  pl.BlockSpec((tm, D_MODEL), prv),
                  pl.BlockSpec((1, 1, D_MODEL), prv_batch),
                  pl.BlockSpec((1, D_MODEL), full),
                  pl.BlockSpec((1, D_MODEL), full),
                  pl.BlockSpec((1, 1, D_MODEL), prv_batch),
                  pl.BlockSpec((1, 1, D_MODEL), prv_batch),
                  pl.BlockSpec((D_MODEL, LANES), full),
                  pl.BlockSpec((1, LANES), full)],
        out_specs=[pl.BlockSpec((tm, D_MODEL), prv),
                   pl.BlockSpec((tm * SLABS, LANES), prv),
                   pl.BlockSpec((tm, LANES), prv)],
        out_shape=[jax.ShapeDtypeStruct((TOKENS, D_MODEL), F32),
                   jax.ShapeDtypeStruct((TOKENS * SLABS, LANES), U32),
                   jax.ShapeDtypeStruct((TOKENS, LANES), F32)],
        scratch_shapes=[pltpu.VMEM((tm, D_MODEL), F32), pltpu.VMEM((tm, D_MODEL), F32)],
        compiler_params=_cparams(("arbitrary",)),
        name="proj",
    )(y, w_bf, x2, gate, ln_g, ln_b, scale, shift, wr, br)


def _proj_ln_router(mix_in, w_bf, x2, gate, ln_g, ln_b, scale, shift, wr, br, *, merge):
    if not merge:
        return _proj_ln_router_pipelined(mix_in[0], w_bf, x2, gate, ln_g, ln_b, scale, shift, wr, br)
    tm = PROJ_TM if merge else 2 * PROJ_TM
    per_b = SEQ // tm
    row = lambda i: (i, 0)
    full = lambda i: (0, 0)
    per_batch = lambda i: (i // per_b, 0, 0)
    scratch = []
    if merge:
        def resid(width, dil):
            return pl.BlockSpec((1, dil, tm // dil, width), lambda i: (i // per_b, 0, i % per_b, 0))
        mix_specs = ([resid(D_MODEL, dil) for dil in DILATIONS] + [resid(LANES, dil) for dil in DILATIONS])
        scratch = [pltpu.VMEM((HEADS, tm, HEAD_DIM), F32), pltpu.VMEM((HEADS, tm, HEAD_DIM), F32),
                   pltpu.VMEM((tm, LANES), F32), pltpu.VMEM((tm, LANES), F32)]
    else:
        mix_specs = [pl.BlockSpec((tm, D_MODEL), row)]
    return pl.pallas_call(
        functools.partial(_proj_kernel, merge=merge),
        grid=(TOKENS // tm,),
        in_specs=mix_specs + [
            pl.BlockSpec((D_MODEL, D_MODEL), full),
            pl.BlockSpec((tm, D_MODEL), row),
            pl.BlockSpec((1, 1, D_MODEL), per_batch),
            pl.BlockSpec((1, D_MODEL), full),
            pl.BlockSpec((1, D_MODEL), full),
            pl.BlockSpec((1, 1, D_MODEL), per_batch),
            pl.BlockSpec((1, 1, D_MODEL), per_batch),
            pl.BlockSpec((D_MODEL, LANES), full),
            pl.BlockSpec((1, LANES), full)],
        out_specs=[pl.BlockSpec((tm, D_MODEL), row),
                   pl.BlockSpec((tm * SLABS, LANES), row),
                   pl.BlockSpec((tm, LANES), row)],
        out_shape=[jax.ShapeDtypeStruct((TOKENS, D_MODEL), F32),
                   jax.ShapeDtypeStruct((TOKENS * SLABS, LANES), U32),
                   jax.ShapeDtypeStruct((TOKENS, LANES), F32)],
        scratch_shapes=scratch,
        compiler_params=_cparams(("arbitrary",)),
        name="proj_merge" if merge else "proj",
    )(*mix_in, w_bf, x2, gate, ln_g, ln_b, scale, shift, wr, br)


def _slab_rows(row):
    if isinstance(row, int):
        return pl.ds(row * SLABS, SLABS)
    return pl.ds(pl.multiple_of(row * SLABS, SLABS), SLABS)


def _row_copy(src, dst, s_row, d_row, sem):
    return pltpu.make_async_copy(src.at[_slab_rows(s_row)], dst.at[_slab_rows(d_row)], sem)


MOE_SLOTS = 3


def _moe_kernel(be_ref, nv_ref, rt_ref, hs_hbm, wg_ref, wu_ref, wd_ref, ys_ref, xbuf, wg_bf, wu_bf, wd_bf, sems):
    s = pl.program_id(0)
    blk = s - 1

    def gather(b):
        slot = b % MOE_SLOTS

        def issue(k, carry):
            _row_copy(hs_hbm, xbuf, rt_ref[b * MOE_BLK + k], slot * MOE_BLK + k, sems.at[slot]).start()
            return carry
        lax.fori_loop(0, MOE_BLK, issue, 0, unroll=8)

    def used(b):
        return (b < MOE_NB) & (nv_ref[jnp.minimum(b, MOE_NB - 1)] > 0)

    @pl.when(s == 0)
    def _():
        for b in range(MOE_SLOTS - 1):
            @pl.when(used(b))
            def _():
                gather(b)

    ahead = blk + MOE_SLOTS - 1

    @pl.when((s > 0) & used(ahead))
    def _():
        gather(ahead)

    nv = nv_ref[jnp.maximum(blk, 0)]

    @pl.when((s > 0) & (nv > 0))
    def _():
        slot = blk % MOE_SLOTS

        def drain(k, carry):
            _row_copy(hs_hbm, xbuf, 0, 0, sems.at[slot]).wait()
            return carry
        lax.fori_loop(0, MOE_BLK, drain, 0, unroll=8)

        x = _load_slabs(xbuf, pl.multiple_of(slot * (MOE_BLK * SLABS), MOE_BLK * SLABS), MOE_BLK)
        rows = lax.broadcasted_iota(jnp.int32, (MOE_BLK, 1), 0)
        x = jnp.where(rows < nv, x, 0.0).astype(BF16)
        g = jnp.dot(x, wg_bf[...], preferred_element_type=F32)
        u = jnp.dot(x, wu_bf[...], preferred_element_type=F32)
        a = (g * (1.0 / (1.0 + jnp.exp(-g)))) * u
        y = jnp.dot(a.astype(BF16), wd_bf[...], preferred_element_type=F32)
        _store_slabs(ys_ref, y)

    @pl.when((s > 0) & (nv <= 0))
    def _():
        ys_ref[...] = jnp.zeros_like(ys_ref)

    new_expert = (s == 0) | (be_ref[jnp.minimum(s, MOE_NB - 1)] != be_ref[jnp.maximum(blk, 0)])

    @pl.when(used(s) & new_expert)
    def _():
        wg_bf[...] = wg_ref[0, 0].astype(BF16)
        wu_bf[...] = wu_ref[0, 0].astype(BF16)
        wd_bf[...] = wd_ref[0, 0].astype(BF16)


def _moe_ffn(block_e, block_nv, row_tok, hs, w_gate, w_up, w_down, layer):
    wmap = lambda s, be, nv, rt: (layer, be[jnp.minimum(s, MOE_NB - 1)], 0, 0)
    return pl.pallas_call(
        _moe_kernel,
        grid_spec=pltpu.PrefetchScalarGridSpec(
            num_scalar_prefetch=3,
            grid=(MOE_NB + 1,),
            in_specs=[pl.BlockSpec(memory_space=pl.ANY),
                      pl.BlockSpec((1, 1, D_MODEL, EXPERT_DIM), wmap),
                      pl.BlockSpec((1, 1, D_MODEL, EXPERT_DIM), wmap),
                      pl.BlockSpec((1, 1, EXPERT_DIM, D_MODEL), wmap)],
            out_specs=pl.BlockSpec((MOE_BLK * SLABS, LANES), lambda s, be, nv, rt: (jnp.maximum(s - 1, 0), 0)),
            scratch_shapes=[pltpu.VMEM((MOE_SLOTS * MOE_BLK * SLABS, LANES), U32),
                            pltpu.VMEM((D_MODEL, EXPERT_DIM), BF16),
                            pltpu.VMEM((D_MODEL, EXPERT_DIM), BF16),
                            pltpu.VMEM((EXPERT_DIM, D_MODEL), BF16),
                            pltpu.SemaphoreType.DMA((MOE_SLOTS,))]),
        out_shape=jax.ShapeDtypeStruct((MOE_ROWS * SLABS, LANES), U32),
        compiler_params=pltpu.CompilerParams(dimension_semantics=("arbitrary",), vmem_limit_bytes=VMEM_LIMIT,
                                             disable_bounds_checks=True),
        name="moe_ffn",
    )(block_e, block_nv, row_tok, hs, w_gate, w_up, w_down)


COMB_TM = 256


def _combine_kernel(dest_ref, ys_hbm, gt_ref, x_ref, gate_ref, g_ref, b_ref, *rest, modulate):
    if modulate:
        sc_ref, sh_ref, xn_ref, hn_ref, buf, sems = rest
    else:
        xn_ref, buf, sems = rest
    i = pl.program_id(0)
    tm = COMB_TM

    def gather(tile, slot):
        def issue(t, carry):
            tok = tile * tm + t
            for k in range(2):
                _row_copy(ys_hbm, buf, dest_ref[2 * tok + k], (2 * slot + k) * tm + t, sems.at[slot]).start()
            return carry
        lax.fori_loop(0, tm, issue, 0, unroll=4)

    @pl.when(i == 0)
    def _():
        gather(0, 0)

    @pl.when(i + 1 < pl.num_programs(0))
    def _():
        gather(i + 1, (i + 1) % 2)

    slot = i % 2

    def drain(t, carry):
        _row_copy(ys_hbm, buf, 0, 0, sems.at[slot]).wait()
        _row_copy(ys_hbm, buf, 0, 0, sems.at[slot]).wait()
        return carry

    lax.fori_loop(0, tm, drain, 0, unroll=4)

    gt = gt_ref[...]
    base = pl.multiple_of(slot * (2 * tm * SLABS), 2 * tm * SLABS)
    y = _load_slabs(buf, base, tm) * gt[:, 0:1] + _load_slabs(buf, base + tm * SLABS, tm) * gt[:, 1:2]
    xn = _deepnorm(x_ref[...], y, gate_ref[0], g_ref[...], b_ref[...])
    xn_ref[...] = xn
    if modulate:
        hn_ref[...] = (xn * (1.0 + sc_ref[0]) + sh_ref[0]).astype(BF16)


def _combine(dest, ys, gates, x2, gate, ln_g, ln_b, scale=None, shift=None):
    tm = COMB_TM
    per_b = SEQ // tm
    modulate = scale is not None
    row = lambda i, d: (i, 0)
    full = lambda i, d: (0, 0)
    per_batch = lambda i, d: (i // per_b, 0, 0)
    in_specs = [pl.BlockSpec(memory_space=pl.ANY),
                pl.BlockSpec((tm, 2), row),
                pl.BlockSpec((tm, D_MODEL), row),
                pl.BlockSpec((1, 1, D_MODEL), per_batch),
                pl.BlockSpec((1, D_MODEL), full),
                pl.BlockSpec((1, D_MODEL), full)]
    out_specs = [pl.BlockSpec((tm, D_MODEL), row)]
    out_shape = [jax.ShapeDtypeStruct((TOKENS, D_MODEL), F32)]
    args = [dest, ys, gates, x2, gate, ln_g, ln_b]
    if modulate:
        in_specs += [pl.BlockSpec((1, 1, D_MODEL), per_batch)] * 2
        out_specs.append(pl.BlockSpec((tm, D_MODEL), row))
        out_shape.append(jax.ShapeDtypeStruct((TOKENS, D_MODEL), BF16))
        args += [scale, shift]
    return pl.pallas_call(
        functools.partial(_combine_kernel, modulate=modulate),
        grid_spec=pltpu.PrefetchScalarGridSpec(
            num_scalar_prefetch=1,
            grid=(TOKENS // tm,),
            in_specs=in_specs,
            out_specs=out_specs,
            scratch_shapes=[pltpu.VMEM((4 * tm * SLABS, LANES), U32), pltpu.SemaphoreType.DMA((2,))]),
        out_shape=out_shape,
        compiler_params=pltpu.CompilerParams(dimension_semantics=("arbitrary",), vmem_limit_bytes=VMEM_LIMIT,
                                             disable_bounds_checks=True),
        name="moe_combine",
    )(*args)


def _route(logits):
    lc = logits[:, :N_GROUPS]
    lf = logits[:, N_GROUPS:N_GROUPS + N_EXPERTS].reshape(TOKENS, N_GROUPS, EPG)
    pc = jax.nn.softmax(lc, axis=-1)
    pg = jnp.max(pc, axis=-1, keepdims=True)
    gi = jnp.argmax(pc, axis=-1)[:, None]
    lf_sel = jnp.take_along_axis(lf, gi[:, :, None], axis=1)[:, 0]
    pf = jax.nn.softmax(lf_sel, axis=-1)
    e1 = jnp.argmax(pf, axis=-1)[:, None]
    p1 = jnp.max(pf, axis=-1, keepdims=True)
    pf2 = jnp.where(jnp.arange(EPG)[None, :] == e1, -1.0, pf)
    e2 = jnp.argmax(pf2, axis=-1)[:, None]
    p2 = jnp.max(pf2, axis=-1, keepdims=True)
    pk = jnp.concatenate([p1, p2], axis=1)
    ek = jnp.concatenate([e1, e2], axis=1)
    gates = pg * (pk / jnp.sum(pk, axis=-1, keepdims=True))
    e_flat = (gi * EPG + ek).reshape(-1).astype(jnp.int32)
    onehot = (e_flat[:, None] == jnp.arange(N_EXPERTS, dtype=jnp.int32)[None, :]).astype(jnp.int32)
    csum = jnp.cumsum(onehot, axis=0)
    rank = jnp.sum((csum - onehot) * onehot, axis=1)
    counts = csum[-1]
    padded = (counts + MOE_BLK - 1) // MOE_BLK * MOE_BLK
    pends = jnp.cumsum(padded)
    pstarts = pends - padded
    dest = (pstarts[e_flat] + rank).astype(jnp.int32)
    blk_start = jnp.arange(MOE_NB, dtype=jnp.int32) * MOE_BLK
    block_e = jnp.clip(jnp.searchsorted(pends, blk_start, side='right'), 0, N_EXPERTS - 1).astype(jnp.int32)
    block_nv = jnp.clip(counts[block_e] - (blk_start - pstarts[block_e]), 0, MOE_BLK)
    block_nv = jnp.where(blk_start < pends[-1], block_nv, 0).astype(jnp.int32)
    row_tok = jnp.zeros((MOE_ROWS,), jnp.int32).at[dest].set(jnp.arange(2 * TOKENS, dtype=jnp.int32) // 2)
    return gates, dest, row_tok, block_e, block_nv


def _moe(hs, logits, x2, gate, ln_g, ln_b, w_gate, w_up, w_down, layer, scale=None, shift=None):
    gates, dest, row_tok, block_e, block_nv = _route(logits)
    ys = _moe_ffn(block_e, block_nv, row_tok, hs, w_gate, w_up, w_down, layer)
    return _combine(dest, ys, gates, x2, gate, ln_g, ln_b, scale, shift)


FN_TM = 512


def _fnet_in_kernel(h_ref, w_ref, cs_ref, ab_ref):
    u = jnp.dot(h_ref[...], w_ref[...], preferred_element_type=F32).astype(BF16)
    for g in range(FNET_GROUPS):
        sl = slice(g * FNET_CH, (g + 1) * FNET_CH)
        ab = jnp.dot(u[:, sl], cs_ref[...], preferred_element_type=F32)
        ab_ref[0, 0, :, sl] = ab[:, :FNET_CH].astype(BF16)
        ab_ref[0, 1, :, sl] = ab[:, FNET_CH:].astype(BF16)


def _fnet_in(h_bf, w_in_bf, cs):
    tm = FN_TM
    per_b = SEQ // tm
    return pl.pallas_call(
        _fnet_in_kernel,
        grid=(TOKENS // tm,),
        in_specs=[pl.BlockSpec((tm, D_MODEL), lambda i: (i, 0)),
                  pl.BlockSpec((D_MODEL, D_MODEL), lambda i: (0, 0)),
                  pl.BlockSpec((FNET_CH, 2 * FNET_CH), lambda i: (0, 0))],
        out_specs=pl.BlockSpec((1, 2, tm, D_MODEL), lambda i: (i // per_b, 0, i % per_b, 0)),
        out_shape=jax.ShapeDtypeStruct((BATCH, 2, SEQ, D_MODEL), BF16),
        compiler_params=_cparams(("arbitrary",)),
        name="fnet_in",
    )(h_bf, w_in_bf, cs)


DFT_R = 4
DFT_Q = SEQ // DFT_R
DFT_TN = 256
DFT_CHUNK = 256
_C4 = (1, 0, -1, 0)
_S4 = (0, 1, 0, -1)


def _signed_sum(terms):
    acc = None
    for sign, v in terms:
        if sign == 0:
            continue
        if acc is None:
            acc = v if sign > 0 else -v
        else:
            acc = acc + v if sign > 0 else acc - v
    return acc


def _dft_kernel(a_ref, b_ref, m_hbm, y_ref, m_scr, ab_scr, y_scr, sem):
    @pl.when((pl.program_id(0) == 0) & (pl.program_id(1) == 0))
    def _():
        cp = pltpu.make_async_copy(m_hbm, m_scr, sem)
        cp.start()
        cp.wait()

    for ch in range(DFT_Q // DFT_CHUNK):
        rows = [slice(m * DFT_Q + ch * DFT_CHUNK, m * DFT_Q + (ch + 1) * DFT_CHUNK) for m in range(DFT_R)]
        a = [a_ref[0, 0, r, :].astype(F32) for r in rows]
        b = [b_ref[0, 0, r, :].astype(F32) for r in rows]
        for rho in range(DFT_R):
            q = [(rho * m) % 4 for m in range(DFT_R)]
            ap = _signed_sum([(_C4[q[m]], a[m]) for m in range(DFT_R)] + [(-_S4[q[m]], b[m]) for m in range(DFT_R)])
            bp = _signed_sum([(_S4[q[m]], a[m]) for m in range(DFT_R)] + [(_C4[q[m]], b[m]) for m in range(DFT_R)])
            ab_scr[rho, ch * DFT_CHUNK:(ch + 1) * DFT_CHUNK, :] = ap.astype(BF16)
            ab_scr[rho, DFT_Q + ch * DFT_CHUNK:DFT_Q + (ch + 1) * DFT_CHUNK, :] = bp.astype(BF16)

    for rho in range(DFT_R):
        y = jnp.dot(m_scr[rho], ab_scr[rho], preferred_element_type=F32)
        for c in range(DFT_TN // LANES):
            y_scr[c, pl.ds(rho, DFT_Q, stride=DFT_R), :] = y[:, c * LANES:(c + 1) * LANES]
    for c in range(DFT_TN // LANES):
        y_ref[0, :, c * LANES:(c + 1) * LANES] = y_scr[c].astype(BF16)


def _seq_dft(mats, ab):
    tn = DFT_TN
    return pl.pallas_call(
        _dft_kernel,
        grid=(BATCH, D_MODEL // tn),
        in_specs=[pl.BlockSpec((1, 1, SEQ, tn), lambda b, j: (b, 0, 0, j)),
                  pl.BlockSpec((1, 1, SEQ, tn), lambda b, j: (b, 1, 0, j)),
                  pl.BlockSpec(memory_space=pl.ANY)],
        out_specs=pl.BlockSpec((1, SEQ, tn), lambda b, j: (b, 0, j)),
        out_shape=jax.ShapeDtypeStruct((BATCH, SEQ, D_MODEL), BF16),
        scratch_shapes=[pltpu.VMEM((DFT_R, DFT_Q, 2 * DFT_Q), BF16),
                        pltpu.VMEM((DFT_R, 2 * DFT_Q, tn), BF16),
                        pltpu.VMEM((tn // LANES, SEQ, LANES), F32),
                        pltpu.SemaphoreType.DMA(())],
        compiler_params=_cparams(("arbitrary", "arbitrary")),
        name="seq_dft",
    )(ab, ab, mats)


def _dft_constants():
    n = np.arange(FNET_CH)
    ang = 2.0 * np.pi * ((n[:, None] * n[None, :]) % FNET_CH) / FNET_CH
    cs = np.concatenate([np.cos(ang), np.sin(ang)], axis=1) / math.sqrt(FNET_CH)
    return jnp.asarray(cs, dtype=BF16)


def _seq_dft_matrices():
    shape = (DFT_R, DFT_Q, DFT_Q)
    rho = lax.broadcasted_iota(jnp.int32, shape, 0)
    kp = lax.broadcasted_iota(jnp.int32, shape, 1)
    jp = lax.broadcasted_iota(jnp.int32, shape, 2)
    ang = (((DFT_R * kp + rho) * jp) & (SEQ - 1)).astype(F32) * (2.0 * math.pi / SEQ)
    norm = 1.0 / math.sqrt(SEQ)
    return jnp.concatenate([(jnp.cos(ang) * norm).astype(BF16), (jnp.sin(ang) * (-norm)).astype(BF16)], axis=2)


def kernel(x, c, positions, ada_w, ada_b, attn_w_qkv, attn_w_o, fnet_w_in, fnet_w_out, ln_g, ln_b,
           router_coarse_w, router_coarse_b, router_fine_w, router_fine_b,
           expert_w_gate, expert_w_up, expert_w_down):
    x2 = x.reshape(TOKENS, D_MODEL)

    c8 = jnp.zeros((8, D_MODEL), F32).at[:BATCH].set(c)
    mod = _adaln(c8, ada_w.reshape(4, D_MODEL, 3 * D_MODEL), ada_b.reshape(4, 1, 3 * D_MODEL))[:, :BATCH]

    def mod_rows(idx):
        m = mod[idx].reshape(BATCH, 3, 1, D_MODEL)
        return m[:, 0], m[:, 1], m[:, 2]

    def router_params(i):
        wr = jnp.zeros((D_MODEL, LANES), F32)
        wr = wr.at[:, :N_GROUPS].set(router_coarse_w[i]).at[:, N_GROUPS:N_GROUPS + N_EXPERTS].set(router_fine_w[i])
        br = jnp.zeros((1, LANES), F32)
        br = br.at[0, :N_GROUPS].set(router_coarse_b[i]).at[0, N_GROUPS:N_GROUPS + N_EXPERTS].set(router_fine_b[i])
        return wr, br

    inv_freq = np.float32(ROPE_THETA) ** (-np.arange(0, ROT_DIM, 2, dtype=np.float32) / np.float32(ROT_DIM))
    invf = np.zeros((1, LANES), np.float32)
    invf[0, :ROT_HALF] = inv_freq
    invf[0, ROT_HALF:ROT_DIM] = inv_freq
    pos_rm = [positions.reshape(BATCH, SEQ // QKV_TM, QKV_TM // dil, dil).transpose(0, 1, 3, 2).reshape(TOKENS)
              for dil in DILATIONS]
    tabs = _rot_tables(jnp.concatenate(pos_rm).reshape(N_DIL * TOKENS, 1), jnp.asarray(invf))

    shift, scale, gate = mod_rows(0)
    outs, lses = [], []
    for g, dil in enumerate(DILATIONS):
        o_g, lse_g = _attention_group(_qkv(x2, scale, shift, attn_w_qkv, tabs, g, dil), dil)
        outs.append(o_g)
        lses.append(lse_g)
    shift2, scale2, gate2 = mod_rows(1)
    wr, br = router_params(0)
    x2, hs, logits = _proj_ln_router(outs + lses, attn_w_o[0].astype(BF16), x2, gate, ln_g[0, 0][None], ln_b[0, 0][None],
                                     scale2, shift2, wr, br, merge=True)
    shift3, scale3, gate3 = mod_rows(2)
    x2, h_bf = _moe(hs, logits, x2, gate2, ln_g[0, 1][None], ln_b[0, 1][None],
                    expert_w_gate, expert_w_up, expert_w_down, 0, scale3, shift3)

    ab = _fnet_in(h_bf, fnet_w_in[0].astype(BF16), _dft_constants())
    y = _seq_dft(_seq_dft_matrices(), ab).reshape(TOKENS, D_MODEL)
    shift4, scale4, gate4 = mod_rows(3)
    wr, br = router_params(1)
    x2, hs, logits = _proj_ln_router([y], fnet_w_out[0].astype(BF16), x2, gate3, ln_g[1, 0][None], ln_b[1, 0][None],
                                     scale4, shift4, wr, br, merge=False)
    (x2,) = _moe(hs, logits, x2, gate4, ln_g[1, 1][None], ln_b[1, 1][None],
                 expert_w_gate, expert_w_up, expert_w_down, 1)
    return x2.reshape(BATCH, SEQ, D_MODEL)
```
